```python
import math
import jax, jax.numpy as jnp
from jax import lax
import numpy as np

D_MODEL = 2048
BATCH = 4
SEQ = 2048
DEPTH = 2
DEC_BATCH = 2
DEC_SEQ = 16384
PAST_LEN = 128

GRID_W = 64
Q_BLOCK = 128
EPS = 1e-6
DIFF_HEADS = 8
DIFF_HD = 64
DIFF_V_HD = 2 * DIFF_HD
DIFF_QK_WIDTH = DIFF_HEADS * 2 * DIFF_HD
DIFF_WIDTH = DIFF_HEADS * DIFF_V_HD
GQA_HEADS = 8
GQA_KV_HEADS = 2
GQA_GROUP = GQA_HEADS // GQA_KV_HEADS
GQA_HD = 128
GQA_WIDTH = GQA_HEADS * GQA_HD
GQA_KV_WIDTH = GQA_KV_HEADS * GQA_HD
ROPE_THETA = 10000.0
N_BRANCH = 2
GATE_WIDTH = N_BRANCH * D_MODEL
N_IN = 3 * DIFF_QK_WIDTH + GQA_WIDTH + 2 * GQA_KV_WIDTH + GATE_WIDTH
D_FF = -(-8 * D_MODEL // (3 * 256)) * 256

kernel_name = "hybrid_diffattn_axialgqa_encoder"


def rms_norm(x, g):
    xf = x.astype(jnp.float32)
    y = xf * lax.rsqrt(jnp.mean(xf * xf, axis=-1, keepdims=True) + EPS)
    return (y * g.astype(jnp.float32)).astype(x.dtype)


def query_blocks(a):
    b, s = a.shape[:2]
    a = a.reshape((b, s // Q_BLOCK, Q_BLOCK) + a.shape[2:])
    return jnp.moveaxis(a, 1, 0)


def merge_blocks(a):
    a = jnp.moveaxis(a, 0, 1)
    return a.reshape((a.shape[0], a.shape[1] * a.shape[2]) + a.shape[3:])


def diff_attention(q, k, v, lam, lambda_init, subln_g):
    b, s = q.shape[:2]
    slopes = jnp.exp2(-8.0 * jnp.arange(1, DIFF_HEADS + 1, dtype=jnp.float32) / DIFF_HEADS)
    key_pos = jnp.arange(s, dtype=jnp.float32)
    scale = DIFF_HD ** -0.5

    def block(args):
        qb, idx = args
        q_pos = (idx * Q_BLOCK + jnp.arange(Q_BLOCK)).astype(jnp.float32)
        bias = -slopes[:, None, None] * jnp.abs(q_pos[:, None] - key_pos[None, :])
        sc = jnp.einsum('bqhmd,bshmd->bmhqs', qb, k,
                        preferred_element_type=jnp.float32) * scale + bias
        p = jax.nn.softmax(sc, axis=-1)
        w = p[:, 0] - lam * p[:, 1]
        return jnp.einsum('bhqs,bshe->bqhe', w.astype(v.dtype), v)

    out = merge_blocks(lax.map(block, (query_blocks(q), jnp.arange(s // Q_BLOCK))))
    out = rms_norm(out, subln_g) * (1.0 - lambda_init)
    return out.reshape(b, s, DIFF_WIDTH)


def axial_rope(x, row, col):
    half = GQA_HD // 2
    quarter = half // 2
    freqs = ROPE_THETA ** (-jnp.arange(quarter, dtype=jnp.float32) / quarter)

    def rot(xa, pos):
        ang = pos[:, None] * freqs[None, :]
        cos = jnp.cos(ang)[None, :, None, :]
        sin = jnp.sin(ang)[None, :, None, :]
        x1 = xa[..., :quarter].astype(jnp.float32)
        x2 = xa[..., quarter:].astype(jnp.float32)
        return jnp.concatenate([x1 * cos - x2 * sin, x1 * sin + x2 * cos], axis=-1)

    return jnp.concatenate([rot(x[..., :half], row), rot(x[..., half:], col)], axis=-1).astype(x.dtype)


def gqa_attention(q, k, v, qn_g, kn_g, row, col):
    b, s = q.shape[:2]
    q = axial_rope(rms_norm(q, qn_g), row, col)
    k = axial_rope(rms_norm(k, kn_g), row, col)
    q = q.reshape(b, s, GQA_KV_HEADS, GQA_GROUP, GQA_HD)
    scale = GQA_HD ** -0.5

    def block(qb):
        sc = jnp.einsum('bqkgd,bskd->bkgqs', qb, k,
                        preferred_element_type=jnp.float32) * scale
        p = jax.nn.softmax(sc, axis=-1)
        return jnp.einsum('bkgqs,bskd->bqkgd', p.astype(v.dtype), v)

    out = merge_blocks(lax.map(block, query_blocks(q)))
    return out.reshape(b, s, GQA_WIDTH)


def mixer_sublayer(h, row, col, lambda_init, w_in, diff_lambda, diff_subln_g,
                   gqa_qnorm_g, gqa_knorm_g, w_proj_a, w_proj_b, w_out):
    b, s, _ = h.shape
    sizes = [DIFF_QK_WIDTH, DIFF_QK_WIDTH, DIFF_WIDTH, GQA_WIDTH, GQA_KV_WIDTH, GQA_KV_WIDTH, GATE_WIDTH]
    points = np.cumsum(sizes)[:-1].tolist()
    z = h @ w_in
    dq, dk, dv, gq, gk, gv, gate = jnp.split(z, points, axis=-1)
    lq1, lk1, lq2, lk2 = diff_lambda.astype(jnp.float32)
    lam = jnp.exp(jnp.sum(lq1 * lk1)) - jnp.exp(jnp.sum(lq2 * lk2)) + lambda_init
    o_a = diff_attention(dq.reshape(b, s, DIFF_HEADS, 2, DIFF_HD),
                         dk.reshape(b, s, DIFF_HEADS, 2, DIFF_HD),
                         dv.reshape(b, s, DIFF_HEADS, DIFF_V_HD),
                         lam, lambda_init, diff_subln_g)
    o_b = gqa_attention(gq.reshape(b, s, GQA_HEADS, GQA_HD),
                        gk.reshape(b, s, GQA_KV_HEADS, GQA_HD),
                        gv.reshape(b, s, GQA_KV_HEADS, GQA_HD),
                        gqa_qnorm_g, gqa_knorm_g, row, col)
    g = jax.nn.sigmoid(gate.astype(jnp.float32)).astype(h.dtype)
    g_a, g_b = g[..., :D_MODEL], g[..., D_MODEL:]
    merged = g_a * (o_a @ w_proj_a) + g_b * (o_b @ w_proj_b)
    return merged @ w_out


def swiglu_ffn(h, w_gate_up, w_down):
    gu = h @ w_gate_up
    return (jax.nn.silu(gu[..., :D_FF]) * gu[..., D_FF:]) @ w_down


def trunk(x, mix_pre_g, w_in, diff_lambda, diff_subln_g, gqa_qnorm_g, gqa_knorm_g,
          w_proj_a, w_proj_b, w_out, mix_post_g, ffn_pre_g, w_gate_up, w_down, ffn_post_g):
    s = x.shape[1]
    rows = s // GRID_W
    row = jnp.repeat(jnp.arange(rows, dtype=jnp.float32), GRID_W)
    col = jnp.tile(jnp.arange(GRID_W, dtype=jnp.float32), rows)
    for l in range(DEPTH):
        lambda_init = 0.8 - 0.6 * math.exp(-0.3 * l)
        h = rms_norm(x, mix_pre_g[l])
        m = mixer_sublayer(h, row, col, lambda_init, w_in[l], diff_lambda[l], diff_subln_g[l],
                           gqa_qnorm_g[l], gqa_knorm_g[l], w_proj_a[l], w_proj_b[l], w_out[l])
        x = x + rms_norm(m, mix_post_g[l])
        h = rms_norm(x, ffn_pre_g[l])
        x = x + rms_norm(swiglu_ffn(h, w_gate_up[l], w_down[l]), ffn_post_g[l])
    return x


def setup_inputs(seed: int = 0) -> dict:
    key = jax.random.key(seed)
    ks = jax.random.split(key, 16)
    f32 = jnp.float32

    def w(k, shape, fan_in):
        return jax.random.normal(k, shape, f32) * fan_in ** -0.5

    def gain(k, shape):
        return 1.0 + 0.05 * jax.random.normal(k, shape, f32)

    return {
        "x_prompt": jax.random.normal(ks[0], (BATCH, SEQ, D_MODEL), f32),
        "x_sample": jax.random.normal(ks[1], (DEC_BATCH, DEC_SEQ, D_MODEL), f32),
        "mix_pre_g": gain(ks[2], (DEPTH, D_MODEL)),
        "w_in": w(ks[3], (DEPTH, D_MODEL, N_IN), D_MODEL),
        "diff_lambda": 0.1 * jax.random.normal(ks[4], (DEPTH, 4, DIFF_HD), f32),
        "diff_subln_g": gain(ks[5], (DEPTH, DIFF_V_HD)),
        "gqa_qnorm_g": gain(ks[6], (DEPTH, GQA_HD)),
        "gqa_knorm_g": gain(ks[7], (DEPTH, GQA_HD)),
        "w_proj_a": w(ks[8], (DEPTH, DIFF_WIDTH, D_MODEL), DIFF_WIDTH),
        "w_proj_b": w(ks[9], (DEPTH, GQA_WIDTH, D_MODEL), GQA_WIDTH),
        "w_out": w(ks[10], (DEPTH, D_MODEL, D_MODEL), D_MODEL),
        "mix_post_g": gain(ks[11], (DEPTH, D_MODEL)),
        "ffn_pre_g": gain(ks[12], (DEPTH, D_MODEL)),
        "w_gate_up": w(ks[13], (DEPTH, D_MODEL, 2 * D_FF), D_MODEL),
        "w_down": w(ks[14], (DEPTH, D_FF, D_MODEL), D_FF),
        "ffn_post_g": gain(ks[15], (DEPTH, D_MODEL)),
    }


def reference(x_prompt, x_sample, mix_pre_g, w_in, diff_lambda, diff_subln_g, gqa_qnorm_g,
              gqa_knorm_g, w_proj_a, w_proj_b, w_out, mix_post_g, ffn_pre_g, w_gate_up,
              w_down, ffn_post_g):
    y_prompt = trunk(x_prompt, mix_pre_g, w_in, diff_lambda, diff_subln_g, gqa_qnorm_g,
                     gqa_knorm_g, w_proj_a, w_proj_b, w_out, mix_post_g, ffn_pre_g,
                     w_gate_up, w_down, ffn_post_g)
    y_sample = trunk(x_sample, mix_pre_g, w_in, diff_lambda, diff_subln_g, gqa_qnorm_g,
                     gqa_knorm_g, w_proj_a, w_proj_b, w_out, mix_post_g, ffn_pre_g,
                     w_gate_up, w_down, ffn_post_g)
    return (y_prompt, y_sample)
```

```python
import functools
import math

import jax
import jax.numpy as jnp
from jax import lax
from jax.experimental import pallas as pl
from jax.experimental.pallas import tpu as pltpu

D_MODEL = 2048
DEPTH = 2
GRID_W = 64
EPS = 1e-6
DIFF_HEADS = 8
DIFF_HD = 64
DIFF_V_HD = 128
DIFF_QK_WIDTH = 1024
DIFF_WIDTH = 1024
GQA_HEADS = 8
GQA_KV_HEADS = 2
GQA_GROUP = GQA_HEADS // GQA_KV_HEADS
GQA_HD = 128
GQA_WIDTH = 1024
GQA_KV_WIDTH = 256
ROPE_THETA = 10000.0
GATE_WIDTH = 2 * D_MODEL
D_FF = 5632

HEAD_LANES = 128
OFF_DQ = 0
OFF_DK = OFF_DQ + DIFF_QK_WIDTH
OFF_DV = OFF_DK + DIFF_QK_WIDTH
OFF_GQ = OFF_DV + DIFF_WIDTH
OFF_GK = OFF_GQ + GQA_WIDTH
OFF_GV = OFF_GK + GQA_KV_WIDTH
OFF_GATE = OFF_GV + GQA_KV_WIDTH

VMEM_LIMIT_BYTES = 52 * 1024 * 1024

BF16 = jnp.bfloat16
F32 = jnp.float32


def _params(n_grid):
    return pltpu.CompilerParams(
        dimension_semantics=("parallel",) * (n_grid - 1) + ("arbitrary",),
        vmem_limit_bytes=VMEM_LIMIT_BYTES)


def _rms(x, g):
    ms = jnp.mean(x * x, axis=-1, keepdims=True)
    return x * lax.rsqrt(ms + EPS) * g


def _prenorm_kernel(x_ref, g_ref, o_ref):
    o_ref[...] = _rms(x_ref[...], g_ref[...]).astype(o_ref.dtype)


def _prenorm(x2d, g, tm=1024):
    t = x2d.shape[0]
    return pl.pallas_call(
        _prenorm_kernel,
        grid=(t // tm,),
        in_specs=[pl.BlockSpec((tm, D_MODEL), lambda i: (i, 0)),
                  pl.BlockSpec((1, D_MODEL), lambda i: (0, 0))],
        out_specs=pl.BlockSpec((tm, D_MODEL), lambda i: (i, 0)),
        out_shape=jax.ShapeDtypeStruct((t, D_MODEL), BF16),
        compiler_params=_params(1),
        name="prenorm",
    )(x2d, g.reshape(1, D_MODEL))


def _rope(y, cos, sin_signed):
    lane = lax.broadcasted_iota(jnp.int32, y.shape, 1)
    up = pltpu.roll(y, 96, 1)
    dn = pltpu.roll(y, 32, 1)
    partner = jnp.where((lane % 64) < 32, up, dn)
    return y * cos + partner * sin_signed


def _seg_heads_kernel(h_ref, w_ref, o_ref, *, scale, transpose, n_heads):
    acc = jnp.dot(h_ref[...], w_ref[...], preferred_element_type=F32)
    for hh in range(n_heads):
        y = acc[:, hh * HEAD_LANES:(hh + 1) * HEAD_LANES]
        if scale != 1.0:
            y = y * scale
        if transpose:
            y = y.T
        o_ref[0, hh] = y.astype(o_ref.dtype)


def _seg_rope_kernel(h_ref, w_ref, g_ref, cos_ref, sin_ref, o_ref, *, scale, transpose, n_heads):
    acc = jnp.dot(h_ref[...], w_ref[...], preferred_element_type=F32)
    cos = cos_ref[...]
    sin = sin_ref[...]
    g = g_ref[...]
    for hh in range(n_heads):
        y = _rms(acc[:, hh * HEAD_LANES:(hh + 1) * HEAD_LANES], g)
        y = _rope(y, cos, sin)
        if scale != 1.0:
            y = y * scale
        if transpose:
            y = y.T
        o_ref[0, hh] = y.astype(o_ref.dtype)


def _seg_gate_kernel(h_ref, w_ref, o_ref):
    acc = jnp.dot(h_ref[...], w_ref[...], preferred_element_type=F32)
    o_ref[...] = jax.nn.sigmoid(acc).astype(o_ref.dtype)


def _proj_heads(h2d, w_in, b, s, col0, ncols, *, transpose, scale=1.0, rope=None, tm=1024, tn=512):
    tn = min(tn, ncols)
    n_heads_blk = tn // HEAD_LANES
    n_heads = ncols // HEAD_LANES
    nsb = s // tm
    grid = (b * nsb, ncols // tn)
    cb0 = col0 // tn
    assert col0 % tn == 0 and s % tm == 0
    in_specs = [pl.BlockSpec((tm, D_MODEL), lambda i, j: (i, 0)),
                pl.BlockSpec((D_MODEL, tn), lambda i, j: (0, cb0 + j))]
    args = [h2d, w_in]
    if rope is None:
        body = functools.partial(_seg_heads_kernel, scale=scale, transpose=transpose, n_heads=n_heads_blk)
    else:
        gain, cos, sin = rope
        in_specs += [pl.BlockSpec((1, HEAD_LANES), lambda i, j: (0, 0)),
                     pl.BlockSpec((tm, HEAD_LANES), lambda i, j: (i % nsb, 0)),
                     pl.BlockSpec((tm, HEAD_LANES), lambda i, j: (i % nsb, 0))]
        args += [gain.reshape(1, HEAD_LANES), cos, sin]
        body = functools.partial(_seg_rope_kernel, scale=scale, transpose=transpose, n_heads=n_heads_blk)
    if transpose:
        out_shape = (b, n_heads, HEAD_LANES, s)
        out_spec = pl.BlockSpec((1, n_heads_blk, HEAD_LANES, tm), lambda i, j: (i // nsb, j, 0, i % nsb))
    else:
        out_shape = (b, n_heads, s, HEAD_LANES)
        out_spec = pl.BlockSpec((1, n_heads_blk, tm, HEAD_LANES), lambda i, j: (i // nsb, j, i % nsb, 0))
    return pl.pallas_call(
        body, grid=grid, in_specs=in_specs, out_specs=out_spec,
        out_shape=jax.ShapeDtypeStruct(out_shape, BF16),
        compiler_params=_params(2),
        name="proj_heads",
    )(*args)


def _proj_gate(h2d, w_in, tm=1024, tn=512):
    t = h2d.shape[0]
    cb0 = OFF_GATE // tn
    return pl.pallas_call(
        _seg_gate_kernel,
        grid=(t // tm, GATE_WIDTH // tn),
        in_specs=[pl.BlockSpec((tm, D_MODEL), lambda i, j: (i, 0)),
                  pl.BlockSpec((D_MODEL, tn), lambda i, j: (0, cb0 + j))],
        out_specs=pl.BlockSpec((tm, tn), lambda i, j: (i, j)),
        out_shape=jax.ShapeDtypeStruct((t, GATE_WIDTH), F32),
        compiler_params=_params(2),
        name="proj_gate",
    )(h2d, w_in)


def _online_softmax_step(s, c, vt, m_ref, l_ref, acc_ref, idx):
    m_old = m_ref[idx]
    m_new = jnp.maximum(m_old, jnp.max(s, axis=0, keepdims=True) + c)
    p = jnp.exp(s - (m_new - c))
    alpha = jnp.exp(m_old - m_new)
    l_ref[idx] = alpha * l_ref[idx] + jnp.sum(p, axis=0, keepdims=True)
    acc_ref[idx] = alpha * acc_ref[idx] + jnp.dot(vt, p.astype(BF16), preferred_element_type=F32)
    m_ref[idx] = m_new


def _diff_attn_kernel(slopes_ref, lam_ref, g_ref, qt_ref, k_ref, vt_ref, o_ref,
                      q_s, bias_s, m_s, l_s, acc_s, *, lambda_init, blk):
    h = pl.program_id(1)
    qi = pl.program_id(2)
    ki = pl.program_id(3)
    slope = slopes_ref[h]
    half = DIFF_HD

    @pl.when(ki == 0)
    def _init():
        zeros = jnp.zeros((half, blk), BF16)
        q_s[0, 0:half, :] = qt_ref[0, 0, 0:half, :]
        q_s[0, half:, :] = zeros
        q_s[1, 0:half, :] = zeros
        q_s[1, half:, :] = qt_ref[0, 0, half:, :]
        rel = (lax.broadcasted_iota(jnp.int32, (blk, blk), 0)
               - lax.broadcasted_iota(jnp.int32, (blk, blk), 1)).astype(F32)
        bias_s[0] = slope * rel
        bias_s[1] = -slope * jnp.abs(rel)
        bias_s[2] = -slope * rel
        m_s[...] = jnp.full(m_s.shape, -jnp.inf, F32)
        l_s[...] = jnp.zeros(l_s.shape, F32)
        acc_s[...] = jnp.zeros(acc_s.shape, F32)

    side = jnp.where(ki < qi, 0, jnp.where(ki == qi, 1, 2))
    bias = bias_s[side]
    c = -slope * (jnp.abs(ki - qi) * blk).astype(F32)
    k = k_ref[0, 0]
    vt = vt_ref[0, 0]
    for mp in range(2):
        s = jnp.dot(k, q_s[mp], preferred_element_type=F32) + bias
        _online_softmax_step(s, c, vt, m_s, l_s, acc_s, mp)

    @pl.when(ki == pl.num_programs(3) - 1)
    def _finalize():
        dl = lam_ref[...]
        lam = (jnp.exp(jnp.sum(dl[0:1] * dl[1:2], axis=-1, keepdims=True))
               - jnp.exp(jnp.sum(dl[2:3] * dl[3:4], axis=-1, keepdims=True)) + lambda_init)
        o = acc_s[0] * (1.0 / l_s[0]) - lam * (acc_s[1] * (1.0 / l_s[1]))
        ms = jnp.mean(o * o, axis=0, keepdims=True)
        y = o * lax.rsqrt(ms + EPS) * g_ref[...] * (1.0 - lambda_init)
        o_ref[0] = y.T.astype(o_ref.dtype)


def _diff_attention(qt, k, vt, slopes, diff_lambda, subln_g, lambda_init, blk=512):
    b, nh, _, s = qt.shape
    nb = s // blk
    body = functools.partial(_diff_attn_kernel, lambda_init=lambda_init, blk=blk)
    return pl.pallas_call(
        body,
        grid=(b, nh, nb, nb),
        in_specs=[pl.BlockSpec(memory_space=pltpu.SMEM),
                  pl.BlockSpec((4, DIFF_HD), lambda bb, hh, qi, ki: (0, 0)),
                  pl.BlockSpec((DIFF_V_HD, 1), lambda bb, hh, qi, ki: (0, 0)),
                  pl.BlockSpec((1, 1, HEAD_LANES, blk), lambda bb, hh, qi, ki: (bb, hh, 0, qi)),
                  pl.BlockSpec((1, 1, blk, HEAD_LANES), lambda bb, hh, qi, ki: (bb, hh, ki, 0)),
                  pl.BlockSpec((1, 1, HEAD_LANES, blk), lambda bb, hh, qi, ki: (bb, hh, 0, ki))],
        out_specs=pl.BlockSpec((1, blk, HEAD_LANES), lambda bb, hh, qi, ki: (bb, qi, hh)),
        out_shape=jax.ShapeDtypeStruct((b, s, nh * HEAD_LANES), BF16),
        scratch_shapes=[pltpu.VMEM((2, HEAD_LANES, blk), BF16),
                        pltpu.VMEM((3, blk, blk), F32),
                        pltpu.VMEM((2, 1, blk), F32),
                        pltpu.VMEM((2, 1, blk), F32),
                        pltpu.VMEM((2, HEAD_LANES, blk), F32)],
        compiler_params=_params(4),
        name="diff_attention",
    )(slopes, diff_lambda, subln_g.reshape(DIFF_V_HD, 1), qt, k, vt)


def _gqa_attn_kernel(qt_ref, k_ref, vt_ref, o_ref, m_s, l_s, acc_s):
    ki = pl.program_id(3)

    @pl.when(ki == 0)
    def _init():
        m_s[...] = jnp.full(m_s.shape, -jnp.inf, F32)
        l_s[...] = jnp.zeros(l_s.shape, F32)
        acc_s[...] = jnp.zeros(acc_s.shape, F32)

    s = jnp.dot(k_ref[0, 0], qt_ref[0, 0], preferred_element_type=F32)
    _online_softmax_step(s, 0.0, vt_ref[0, 0], m_s, l_s, acc_s, 0)

    @pl.when(ki == pl.num_programs(3) - 1)
    def _finalize():
        o = acc_s[0] * (1.0 / l_s[0])
        o_ref[0] = o.T.astype(o_ref.dtype)


def _gqa_attention(qt, k, vt, blk=512):
    b, nh, _, s = qt.shape
    nb = s // blk
    return pl.pallas_call(
        _gqa_attn_kernel,
        grid=(b, nh, nb, nb),
        in_specs=[pl.BlockSpec((1, 1, HEAD_LANES, blk), lambda bb, hh, qi, ki: (bb, hh, 0, qi)),
                  pl.BlockSpec((1, 1, blk, HEAD_LANES), lambda bb, hh, qi, ki: (bb, hh // GQA_GROUP, ki, 0)),
                  pl.BlockSpec((1, 1, HEAD_LANES, blk), lambda bb, hh, qi, ki: (bb, hh // GQA_GROUP, 0, ki))],
        out_specs=pl.BlockSpec((1, blk, HEAD_LANES), lambda bb, hh, qi, ki: (bb, qi, hh)),
        out_shape=jax.ShapeDtypeStruct((b, s, nh * HEAD_LANES), BF16),
        scratch_shapes=[pltpu.VMEM((1, 1, blk), F32),
                        pltpu.VMEM((1, 1, blk), F32),
                        pltpu.VMEM((1, HEAD_LANES, blk), F32)],
        compiler_params=_params(4),
        name="gqa_attention",
    )(qt, k, vt)


def _merge_kernel(oa_ref, ob_ref, wa_ref, wb_ref, ga_ref, gb_ref, o_ref):
    a = jnp.dot(oa_ref[...], wa_ref[...], preferred_element_type=F32)
    bb = jnp.dot(ob_ref[...], wb_ref[...], preferred_element_type=F32)
    o_ref[...] = (ga_ref[...] * a + gb_ref[...] * bb).astype(o_ref.dtype)


def _merge(oa, ob, wa, wb, gate, tm=1024, tn=512):
    t = oa.shape[0]
    gb0 = D_MODEL // tn
    return pl.pallas_call(
        _merge_kernel,
        grid=(t // tm, D_MODEL // tn),
        in_specs=[pl.BlockSpec((tm, DIFF_WIDTH), lambda i, j: (i, 0)),
                  pl.BlockSpec((tm, GQA_WIDTH), lambda i, j: (i, 0)),
                  pl.BlockSpec((DIFF_WIDTH, tn), lambda i, j: (0, j)),
                  pl.BlockSpec((GQA_WIDTH, tn), lambda i, j: (0, j)),
                  pl.BlockSpec((tm, tn), lambda i, j: (i, j)),
                  pl.BlockSpec((tm, tn), lambda i, j: (i, gb0 + j))],
        out_specs=pl.BlockSpec((tm, tn), lambda i, j: (i, j)),
        out_shape=jax.ShapeDtypeStruct((t, D_MODEL), BF16),
        compiler_params=_params(2),
        name="merge",
    )(oa, ob, wa, wb, gate, gate)


def _out_proj_kernel(a_ref, w_ref, x_ref, g_ref, o_ref):
    m = jnp.dot(a_ref[...], w_ref[...], preferred_element_type=F32)
    o_ref[...] = x_ref[...] + _rms(m, g_ref[...])


def _out_proj(a, w, x2d, g, tm=512):
    t = a.shape[0]
    return pl.pallas_call(
        _out_proj_kernel,
        grid=(t // tm,),
        in_specs=[pl.BlockSpec((tm, D_MODEL), lambda i: (i, 0)),
                  pl.BlockSpec((D_MODEL, D_MODEL), lambda i: (0, 0)),
                  pl.BlockSpec((tm, D_MODEL), lambda i: (i, 0)),
                  pl.BlockSpec((1, D_MODEL), lambda i: (0, 0))],
        out_specs=pl.BlockSpec((tm, D_MODEL), lambda i: (i, 0)),
        out_shape=jax.ShapeDtypeStruct((t, D_MODEL), F32),
        compiler_params=_params(1),
        name="out_proj",
    )(a, w, x2d, g.reshape(1, D_MODEL))


def _swiglu_kernel(h_ref, wg_ref, wu_ref, o_ref):
    h = h_ref[...]
    gte = jnp.dot(h, wg_ref[...], preferred_element_type=F32)
    up = jnp.dot(h, wu_ref[...], preferred_element_type=F32)
    o_ref[...] = (jax.nn.silu(gte) * up).astype(o_ref.dtype)


def _swiglu(h2d, w_gate_up, tm=1024, tn=512):
    t = h2d.shape[0]
    ub0 = D_FF // tn
    return pl.pallas_call(
        _swiglu_kernel,
        grid=(t // tm, D_FF // tn),
        in_specs=[pl.BlockSpec((tm, D_MODEL), lambda i, j: (i, 0)),
                  pl.BlockSpec((D_MODEL, tn), lambda i, j: (0, j)),
                  pl.BlockSpec((D_MODEL, tn), lambda i, j: (0, ub0 + j))],
        out_specs=pl.BlockSpec((tm, tn), lambda i, j: (i, j)),
        out_shape=jax.ShapeDtypeStruct((t, D_FF), BF16),
        compiler_params=_params(2),
        name="swiglu",
    )(h2d, w_gate_up, w_gate_up)


def _down_kernel(a_ref, w_ref, x_ref, g_ref, o_ref, acc_ref):
    kk = pl.program_id(1)

    @pl.when(kk == 0)
    def _init():
        acc_ref[...] = jnp.zeros(acc_ref.shape, F32)

    acc_ref[...] += jnp.dot(a_ref[...], w_ref[...], preferred_element_type=F32)

    @pl.when(kk == pl.num_programs(1) - 1)
    def _finalize():
        o_ref[...] = x_ref[...] + _rms(acc_ref[...], g_ref[...])


def _down_proj(a, w, x2d, g, tm=512, tk=512):
    t = a.shape[0]
    return pl.pallas_call(
        _down_kernel,
        grid=(t // tm, D_FF // tk),
        in_specs=[pl.BlockSpec((tm, tk), lambda i, kk: (i, kk)),
                  pl.BlockSpec((tk, D_MODEL), lambda i, kk: (kk, 0)),
                  pl.BlockSpec((tm, D_MODEL), lambda i, kk: (i, 0)),
                  pl.BlockSpec((1, D_MODEL), lambda i, kk: (0, 0))],
        out_specs=pl.BlockSpec((tm, D_MODEL), lambda i, kk: (i, 0)),
        out_shape=jax.ShapeDtypeStruct((t, D_MODEL), F32),
        scratch_shapes=[pltpu.VMEM((tm, D_MODEL), F32)],
        compiler_params=_params(2),
        name="down_proj",
    )(a, w, x2d, g.reshape(1, D_MODEL))


def _rope_tables(s):
    rows = s // GRID_W
    row = jnp.repeat(jnp.arange(rows, dtype=F32), GRID_W)
    col = jnp.tile(jnp.arange(GRID_W, dtype=F32), rows)
    quarter = GQA_HD // 4
    freqs = ROPE_THETA ** (-jnp.arange(quarter, dtype=F32) / quarter)
    ang_r = row[:, None] * freqs[None, :]
    ang_c = col[:, None] * freqs[None, :]
    cos = jnp.concatenate([jnp.cos(ang_r), jnp.cos(ang_r), jnp.cos(ang_c), jnp.cos(ang_c)], axis=-1)
    sin = jnp.concatenate([-jnp.sin(ang_r), jnp.sin(ang_r), -jnp.sin(ang_c), jnp.sin(ang_c)], axis=-1)
    return cos, sin


def _trunk(x, weights):
    b, s, _ = x.shape
    x2d = x.reshape(b * s, D_MODEL)
    cos, sin = _rope_tables(s)
    slopes = jnp.exp2(-8.0 * jnp.arange(1, DIFF_HEADS + 1, dtype=F32) / DIFF_HEADS)
    for l in range(DEPTH):
        (mix_pre_g, w_in, diff_lambda, diff_subln_g, gqa_qnorm_g, gqa_knorm_g, w_proj_a, w_proj_b,
         w_out, mix_post_g, ffn_pre_g, w_gate_up, w_down, ffn_post_g) = [w[l] for w in weights]
        lambda_init = 0.8 - 0.6 * math.exp(-0.3 * l)
        h = _prenorm(x2d, mix_pre_g)
        dqt = _proj_heads(h, w_in, b, s, OFF_DQ, DIFF_QK_WIDTH, transpose=True, scale=DIFF_HD ** -0.5)
        dk = _proj_heads(h, w_in, b, s, OFF_DK, DIFF_QK_WIDTH, transpose=False)
        dvt = _proj_heads(h, w_in, b, s, OFF_DV, DIFF_WIDTH, transpose=True)
        o_a = _diff_attention(dqt, dk, dvt, slopes, diff_lambda, diff_subln_g, lambda_init)
        gqt = _proj_heads(h, w_in, b, s, OFF_GQ, GQA_WIDTH, transpose=True, scale=GQA_HD ** -0.5,
                          rope=(gqa_qnorm_g, cos, sin))
        gk = _proj_heads(h, w_in, b, s, OFF_GK, GQA_KV_WIDTH, transpose=False,
                         rope=(gqa_knorm_g, cos, sin))
        gvt = _proj_heads(h, w_in, b, s, OFF_GV, GQA_KV_WIDTH, transpose=True)
        o_b = _gqa_attention(gqt, gk, gvt)
        gate = _proj_gate(h, w_in)
        merged = _merge(o_a.reshape(b * s, DIFF_WIDTH), o_b.reshape(b * s, GQA_WIDTH),
                        w_proj_a, w_proj_b, gate)
        x2d = _out_proj(merged, w_out, x2d, mix_post_g)
        h = _prenorm(x2d, ffn_pre_g)
        act = _swiglu(h, w_gate_up)
        x2d = _down_proj(act, w_down, x2d, ffn_post_g)
    return x2d.reshape(b, s, D_MODEL)


def kernel(x_prompt, x_sample, mix_pre_g, w_in, diff_lambda, diff_subln_g, gqa_qnorm_g, gqa_knorm_g,
           w_proj_a, w_proj_b, w_out, mix_post_g, ffn_pre_g, w_gate_up, w_down, ffn_post_g):
    weights = (mix_pre_g, w_in.astype(BF16), diff_lambda, diff_subln_g, gqa_qnorm_g, gqa_knorm_g,
               w_proj_a.astype(BF16), w_proj_b.astype(BF16), w_out.astype(BF16), mix_post_g,
               ffn_pre_g, w_gate_up.astype(BF16), w_down.astype(BF16), ffn_post_g)
    return (_trunk(x_prompt, weights), _trunk(x_sample, weights))
```

```python
import functools
import math

import jax
import jax.numpy as jnp
from jax import lax
from jax.experimental import pallas as pl
from jax.experimental.pallas import tpu as pltpu

D_MODEL = 2048
DEPTH = 2
GRID_W = 64
EPS = 1e-6
DIFF_HEADS = 8
DIFF_HD = 64
DIFF_V_HD = 128
DIFF_QK_WIDTH = 1024
DIFF_WIDTH = 1024
GQA_HEADS = 8
GQA_KV_HEADS = 2
GQA_GROUP = GQA_HEADS // GQA_KV_HEADS
GQA_HD = 128
GQA_WIDTH = 1024
GQA_KV_WIDTH = 256
ROPE_THETA = 10000.0
GATE_WIDTH = 2 * D_MODEL
D_FF = 5632

HEAD_LANES = 128
OFF_DQ = 0
OFF_DK = OFF_DQ + DIFF_QK_WIDTH
OFF_DV = OFF_DK + DIFF_QK_WIDTH
OFF_GQ = OFF_DV + DIFF_WIDTH
OFF_GK = OFF_GQ + GQA_WIDTH
OFF_GV = OFF_GK + GQA_KV_WIDTH
OFF_GATE = OFF_GV + GQA_KV_WIDTH

VMEM_LIMIT_BYTES = 52 * 1024 * 1024
ATT_BLK = 512
ATT_PANEL = 512
ATT_LOOKAHEAD = 2
LOG2E = math.log2(math.e)

BF16 = jnp.bfloat16
F32 = jnp.float32


def _params(n_grid):
    return pltpu.CompilerParams(
        dimension_semantics=("parallel",) * (n_grid - 1) + ("arbitrary",),
        vmem_limit_bytes=VMEM_LIMIT_BYTES)


def _rms(x, g):
    ms = jnp.mean(x * x, axis=-1, keepdims=True)
    return x * lax.rsqrt(ms + EPS) * g


def _prenorm_kernel(x_ref, g_ref, o_ref):
    o_ref[...] = _rms(x_ref[...], g_ref[...]).astype(o_ref.dtype)


def _prenorm(x2d, g, tm=1024):
    t = x2d.shape[0]
    return pl.pallas_call(
        _prenorm_kernel,
        grid=(t // tm,),
        in_specs=[pl.BlockSpec((tm, D_MODEL), lambda i: (i, 0)),
                  pl.BlockSpec((1, D_MODEL), lambda i: (0, 0))],
        out_specs=pl.BlockSpec((tm, D_MODEL), lambda i: (i, 0)),
        out_shape=jax.ShapeDtypeStruct((t, D_MODEL), BF16),
        compiler_params=_params(1),
        name="prenorm",
    )(x2d, g.reshape(1, D_MODEL))


def _rope(y, cos, sin_signed):
    lane = lax.broadcasted_iota(jnp.int32, y.shape, 1)
    up = pltpu.roll(y, 96, 1)
    dn = pltpu.roll(y, 32, 1)
    partner = jnp.where((lane % 64) < 32, up, dn)
    return y * cos + partner * sin_signed


def _store_head(o_ref, hh, y, transpose):
    if transpose:
        for c in range(y.shape[0] // ATT_BLK):
            o_ref[0, hh, c] = y[c * ATT_BLK:(c + 1) * ATT_BLK].T.astype(o_ref.dtype)
    else:
        o_ref[0, hh] = y.astype(o_ref.dtype)


def _seg_heads_kernel(h_ref, w_ref, o_ref, *, scale, transpose, n_heads):
    acc = jnp.dot(h_ref[...], w_ref[...], preferred_element_type=F32)
    for hh in range(n_heads):
        y = acc[:, hh * HEAD_LANES:(hh + 1) * HEAD_LANES]
        if scale != 1.0:
            y = y * scale
        _store_head(o_ref, hh, y, transpose)


def _seg_rope_kernel(h_ref, w_ref, g_ref, cos_ref, sin_ref, o_ref, *, scale, transpose, n_heads):
    acc = jnp.dot(h_ref[...], w_ref[...], preferred_element_type=F32)
    cos = cos_ref[...]
    sin = sin_ref[...]
    g = g_ref[...]
    for hh in range(n_heads):
        y = _rms(acc[:, hh * HEAD_LANES:(hh + 1) * HEAD_LANES], g)
        y = _rope(y, cos, sin)
        if scale != 1.0:
            y = y * scale
        _store_head(o_ref, hh, y, transpose)


def _seg_gate_kernel(h_ref, w_ref, o_ref):
    acc = jnp.dot(h_ref[...], w_ref[...], preferred_element_type=F32)
    o_ref[...] = jax.nn.sigmoid(acc).astype(o_ref.dtype)


def _proj_heads(h2d, w_in, b, s, col0, ncols, *, transpose, scale=1.0, rope=None, tm=1024, tn=512):
    tn = min(tn, ncols)
    n_heads_blk = tn // HEAD_LANES
    n_heads = ncols // HEAD_LANES
    nsb = s // tm
    grid = (b * nsb, ncols // tn)
    cb0 = col0 // tn
    assert col0 % tn == 0 and s % tm == 0
    in_specs = [pl.BlockSpec((tm, D_MODEL), lambda i, j: (i, 0)),
                pl.BlockSpec((D_MODEL, tn), lambda i, j: (0, cb0 + j))]
    args = [h2d, w_in]
    if rope is None:
        body = functools.partial(_seg_heads_kernel, scale=scale, transpose=transpose, n_heads=n_heads_blk)
    else:
        gain, cos, sin = rope
        in_specs += [pl.BlockSpec((1, HEAD_LANES), lambda i, j: (0, 0)),
                     pl.BlockSpec((tm, HEAD_LANES), lambda i, j: (i % nsb, 0)),
                     pl.BlockSpec((tm, HEAD_LANES), lambda i, j: (i % nsb, 0))]
        args += [gain.reshape(1, HEAD_LANES), cos, sin]
        body = functools.partial(_seg_rope_kernel, scale=scale, transpose=transpose, n_heads=n_heads_blk)
    if transpose:
        out_shape = (b, n_heads, s // ATT_BLK, HEAD_LANES, ATT_BLK)
        out_spec = pl.BlockSpec((1, n_heads_blk, tm // ATT_BLK, HEAD_LANES, ATT_BLK),
                                lambda i, j: (i // nsb, j, i % nsb, 0, 0))
    else:
        out_shape = (b, n_heads, s, HEAD_LANES)
        out_spec = pl.BlockSpec((1, n_heads_blk, tm, HEAD_LANES), lambda i, j: (i // nsb, j, i % nsb, 0))
    return pl.pallas_call(
        body, grid=grid, in_specs=in_specs, out_specs=out_spec,
        out_shape=jax.ShapeDtypeStruct(out_shape, BF16),
        compiler_params=_params(2),
        name="proj_heads",
    )(*args)


def _proj_gate(h2d, w_in, tm=1024, tn=512):
    t = h2d.shape[0]
    cb0 = OFF_GATE // tn
    return pl.pallas_call(
        _seg_gate_kernel,
        grid=(t // tm, GATE_WIDTH // tn),
        in_specs=[pl.BlockSpec((tm, D_MODEL), lambda i, j: (i, 0)),
                  pl.BlockSpec((D_MODEL, tn), lambda i, j: (0, cb0 + j))],
        out_specs=pl.BlockSpec((tm, tn), lambda i, j: (i, j)),
        out_shape=jax.ShapeDtypeStruct((t, GATE_WIDTH), F32),
        compiler_params=_params(2),
        name="proj_gate",
    )(h2d, w_in)


def _online_softmax_step(s, c, vt, m_ref, l_ref, acc_ref, idx):
    m_old = m_ref[idx]
    m_new = jnp.maximum(m_old, jnp.max(s, axis=0, keepdims=True) + c)
    p = jnp.exp2(s - (m_new - c))
    alpha = jnp.exp2(m_old - m_new)
    l_ref[idx] = alpha * l_ref[idx] + jnp.sum(p, axis=0, keepdims=True)
    acc_ref[idx] = alpha * acc_ref[idx] + jnp.dot(vt, p.astype(BF16), preferred_element_type=F32)
    m_ref[idx] = m_new


def _sweep_key_blocks(n_kblk, keys, values_t, score, step, n_chains, s_s):
    ahead = ATT_LOOKAHEAD
    k0 = keys(0)
    for i in range(ahead):
        s_s[i] = score(k0, i)

    def key_block(j, carry):
        k = keys(j)
        k_next = keys(jnp.minimum(j + 1, n_kblk - 1))
        vt = values_t(j)
        queue = [s_s[i] for i in range(ahead)]
        for ci in range(n_chains):
            nxt = ci + ahead
            queue.append(score(k, nxt) if nxt < n_chains else score(k_next, nxt - n_chains))
            step(queue.pop(0), ci, j, vt)
        for i in range(ahead):
            s_s[i] = queue[i]
        return carry

    lax.fori_loop(0, n_kblk, key_block, 0)


def _diff_attn_kernel(slopes_ref, lam_ref, g_ref, qt_ref, k_ref, vt_ref, o_ref,
                      q_s, bias_s, s_s, m_s, l_s, acc_s, *, lambda_init, n_kblk):
    blk = ATT_BLK
    h = pl.program_id(1)
    qi = pl.program_id(2)
    slope = slopes_ref[h] * LOG2E
    half = DIFF_HD

    @pl.when(qi == 0)
    def _head_init():
        rel = (lax.broadcasted_iota(jnp.int32, (blk, blk), 0)
               - lax.broadcasted_iota(jnp.int32, (blk, blk), 1)).astype(F32)
        bias_s[0] = slope * rel
        bias_s[1] = -slope * jnp.abs(rel)
        bias_s[2] = -slope * rel

    zeros = jnp.zeros((half, blk), BF16)
    q_s[0, 0:half, :] = qt_ref[0, 0, 0, 0:half, :]
    q_s[0, half:, :] = zeros
    q_s[1, 0:half, :] = zeros
    q_s[1, half:, :] = qt_ref[0, 0, 0, half:, :]
    m_s[...] = jnp.full(m_s.shape, -jnp.inf, F32)
    l_s[...] = jnp.zeros(l_s.shape, F32)
    acc_s[...] = jnp.zeros(acc_s.shape, F32)

    def keys(j):
        return k_ref[0, 0, pl.ds(pl.multiple_of(j * blk, blk), blk), :]

    pw = ATT_PANEL
    n_panels = blk // pw

    def score(kblk, ci):
        mp, pn = divmod(ci, n_panels)
        return jnp.dot(kblk, q_s[mp, :, pn * pw:(pn + 1) * pw], preferred_element_type=F32)

    def step(s, ci, j, vt):
        pn = ci % n_panels
        side = jnp.where(j < qi, 0, jnp.where(j == qi, 1, 2))
        c = -slope * (jnp.abs(j - qi) * blk).astype(F32)
        _online_softmax_step(s + bias_s[side, :, pn * pw:(pn + 1) * pw], c, vt, m_s, l_s, acc_s, ci)

    _sweep_key_blocks(n_kblk, keys, lambda j: vt_ref[0, 0, j], score, step, 2 * n_panels, s_s)

    dl = lam_ref[...]
    lam = (jnp.exp(jnp.sum(dl[0:1] * dl[1:2], axis=-1, keepdims=True))
           - jnp.exp(jnp.sum(dl[2:3] * dl[3:4], axis=-1, keepdims=True)) + lambda_init)
    for pn in range(n_panels):
        c0, c1 = pn, n_panels + pn
        o = acc_s[c0] * (1.0 / l_s[c0]) - lam * (acc_s[c1] * (1.0 / l_s[c1]))
        ms = jnp.mean(o * o, axis=0, keepdims=True)
        y = o * lax.rsqrt(ms + EPS) * g_ref[...] * (1.0 - lambda_init)
        o_ref[0, pn * pw:(pn + 1) * pw, :] = y.T.astype(o_ref.dtype)


def _diff_attention(qt, k, vt, slopes, diff_lambda, subln_g, lambda_init):
    b, nh, nb, _, blk = qt.shape
    s = nb * blk
    body = functools.partial(_diff_attn_kernel, lambda_init=lambda_init, n_kblk=nb)
    return pl.pallas_call(
        body,
        grid=(b, nh, nb),
        in_specs=[pl.BlockSpec(memory_space=pltpu.SMEM),
                  pl.BlockSpec((4, DIFF_HD), lambda bb, hh, qi: (0, 0)),
                  pl.BlockSpec((DIFF_V_HD, 1), lambda bb, hh, qi: (0, 0)),
                  pl.BlockSpec((1, 1, 1, HEAD_LANES, blk), lambda bb, hh, qi: (bb, hh, qi, 0, 0)),
                  pl.BlockSpec((1, 1, s, HEAD_LANES), lambda bb, hh, qi: (bb, hh, 0, 0)),
                  pl.BlockSpec((1, 1, nb, HEAD_LANES, blk), lambda bb, hh, qi: (bb, hh, 0, 0, 0))],
        out_specs=pl.BlockSpec((1, blk, HEAD_LANES), lambda bb, hh, qi: (bb, qi, hh)),
        out_shape=jax.ShapeDtypeStruct((b, s, nh * HEAD_LANES), BF16),
        scratch_shapes=[pltpu.VMEM((2, HEAD_LANES, blk), BF16),
                        pltpu.VMEM((3, blk, blk), F32),
                        pltpu.VMEM((ATT_LOOKAHEAD, blk, ATT_PANEL), F32),
                        pltpu.VMEM((2 * blk // ATT_PANEL, 1, ATT_PANEL), F32),
                        pltpu.VMEM((2 * blk // ATT_PANEL, 1, ATT_PANEL), F32),
                        pltpu.VMEM((2 * blk // ATT_PANEL, HEAD_LANES, ATT_PANEL), F32)],
        compiler_params=pltpu.CompilerParams(dimension_semantics=("arbitrary",) * 3,
                                             vmem_limit_bytes=VMEM_LIMIT_BYTES),
        name="diff_attention",
    )(slopes, diff_lambda, subln_g.reshape(DIFF_V_HD, 1), qt, k, vt)


def _gqa_attn_kernel(qt_ref, k_ref, vt_ref, o_ref, s_s, m_s, l_s, acc_s, *, n_kblk):
    blk = ATT_BLK
    m_s[...] = jnp.full(m_s.shape, -jnp.inf, F32)
    l_s[...] = jnp.zeros(l_s.shape, F32)
    acc_s[...] = jnp.zeros(acc_s.shape, F32)

    def keys(j):
        return k_ref[0, 0, pl.ds(pl.multiple_of(j * blk, blk), blk), :]

    pw = ATT_PANEL
    n_panels = blk // pw

    def score(kblk, ci):
        g, pn = divmod(ci, n_panels)
        return jnp.dot(kblk, qt_ref[0, g, 0, :, pn * pw:(pn + 1) * pw], preferred_element_type=F32)

    def step(s, ci, j, vt):
        _online_softmax_step(s, 0.0, vt, m_s, l_s, acc_s, ci)

    _sweep_key_blocks(n_kblk, keys, lambda j: vt_ref[0, 0, j], score, step, GQA_GROUP * n_panels, s_s)

    for ci in range(GQA_GROUP * n_panels):
        g, pn = divmod(ci, n_panels)
        o = acc_s[ci] * (1.0 / l_s[ci])
        o_ref[0, pn * pw:(pn + 1) * pw, g * HEAD_LANES:(g + 1) * HEAD_LANES] = o.T.astype(o_ref.dtype)


def _gqa_attention(qt, k, vt):
    b, nh, nb, _, blk = qt.shape
    s = nb * blk
    nkv = nh // GQA_GROUP
    body = functools.partial(_gqa_attn_kernel, n_kblk=nb)
    return pl.pallas_call(
        body,
        grid=(b, nkv, nb),
        in_specs=[pl.BlockSpec((1, GQA_GROUP, 1, HEAD_LANES, blk), lambda bb, kv, qi: (bb, kv, qi, 0, 0)),
                  pl.BlockSpec((1, 1, s, HEAD_LANES), lambda bb, kv, qi: (bb, kv, 0, 0)),
                  pl.BlockSpec((1, 1, nb, HEAD_LANES, blk), lambda bb, kv, qi: (bb, kv, 0, 0, 0))],
        out_specs=pl.BlockSpec((1, blk, GQA_GROUP * HEAD_LANES), lambda bb, kv, qi: (bb, qi, kv)),
        out_shape=jax.ShapeDtypeStruct((b, s, nh * HEAD_LANES), BF16),
        scratch_shapes=[pltpu.VMEM((ATT_LOOKAHEAD, blk, ATT_PANEL), F32),
                        pltpu.VMEM((GQA_GROUP * blk // ATT_PANEL, 1, ATT_PANEL), F32),
                        pltpu.VMEM((GQA_GROUP * blk // ATT_PANEL, 1, ATT_PANEL), F32),
                        pltpu.VMEM((GQA_GROUP * blk // ATT_PANEL, HEAD_LANES, ATT_PANEL), F32)],
        compiler_params=_params(3),
        name="gqa_attention",
    )(qt, k, vt)


def _merge_kernel(oa_ref, ob_ref, wa_ref, wb_ref, ga_ref, gb_ref, o_ref):
    a = jnp.dot(oa_ref[...], wa_ref[...], preferred_element_type=F32)
    bb = jnp.dot(ob_ref[...], wb_ref[...], preferred_element_type=F32)
    o_ref[...] = (ga_ref[...] * a + gb_ref[...] * bb).astype(o_ref.dtype)


def _merge(oa, ob, wa, wb, gate, tm=1024, tn=512):
    t = oa.shape[0]
    gb0 = D_MODEL // tn
    return pl.pallas_call(
        _merge_kernel,
        grid=(t // tm, D_MODEL // tn),
        in_specs=[pl.BlockSpec((tm, DIFF_WIDTH), lambda i, j: (i, 0)),
                  pl.BlockSpec((tm, GQA_WIDTH), lambda i, j: (i, 0)),
                  pl.BlockSpec((DIFF_WIDTH, tn), lambda i, j: (0, j)),
                  pl.BlockSpec((GQA_WIDTH, tn), lambda i, j: (0, j)),
                  pl.BlockSpec((tm, tn), lambda i, j: (i, j)),
                  pl.BlockSpec((tm, tn), lambda i, j: (i, gb0 + j))],
        out_specs=pl.BlockSpec((tm, tn), lambda i, j: (i, j)),
        out_shape=jax.ShapeDtypeStruct((t, D_MODEL), BF16),
        compiler_params=_params(2),
        name="merge",
    )(oa, ob, wa, wb, gate, gate)


def _out_proj_kernel(a_ref, w_ref, x_ref, g_ref, o_ref):
    m = jnp.dot(a_ref[...], w_ref[...], preferred_element_type=F32)
    o_ref[...] = x_ref[...] + _rms(m, g_ref[...])


def _out_proj(a, w, x2d, g, tm=512):
    t = a.shape[0]
    return pl.pallas_call(
        _out_proj_kernel,
        grid=(t // tm,),
        in_specs=[pl.BlockSpec((tm, D_MODEL), lambda i: (i, 0)),
                  pl.BlockSpec((D_MODEL, D_MODEL), lambda i: (0, 0)),
                  pl.BlockSpec((tm, D_MODEL), lambda i: (i, 0)),
                  pl.BlockSpec((1, D_MODEL), lambda i: (0, 0))],
        out_specs=pl.BlockSpec((tm, D_MODEL), lambda i: (i, 0)),
        out_shape=jax.ShapeDtypeStruct((t, D_MODEL), F32),
        compiler_params=_params(1),
        name="out_proj",
    )(a, w, x2d, g.reshape(1, D_MODEL))


def _swiglu_kernel(h_ref, wg_ref, wu_ref, o_ref):
    h = h_ref[...]
    gte = jnp.dot(h, wg_ref[...], preferred_element_type=F32)
    up = jnp.dot(h, wu_ref[...], preferred_element_type=F32)
    o_ref[...] = (jax.nn.silu(gte) * up).astype(o_ref.dtype)


def _swiglu(h2d, w_gate_up, tm=1024, tn=512):
    t = h2d.shape[0]
    ub0 = D_FF // tn
    return pl.pallas_call(
        _swiglu_kernel,
        grid=(t // tm, D_FF // tn),
        in_specs=[pl.BlockSpec((tm, D_MODEL), lambda i, j: (i, 0)),
                  pl.BlockSpec((D_MODEL, tn), lambda i, j: (0, j)),
                  pl.BlockSpec((D_MODEL, tn), lambda i, j: (0, ub0 + j))],
        out_specs=pl.BlockSpec((tm, tn), lambda i, j: (i, j)),
        out_shape=jax.ShapeDtypeStruct((t, D_FF), BF16),
        compiler_params=_params(2),
        name="swiglu",
    )(h2d, w_gate_up, w_gate_up)


def _down_kernel(a_ref, w_ref, x_ref, g_ref, o_ref, acc_ref):
    kk = pl.program_id(1)

    @pl.when(kk == 0)
    def _init():
        acc_ref[...] = jnp.zeros(acc_ref.shape, F32)

    acc_ref[...] += jnp.dot(a_ref[...], w_ref[...], preferred_element_type=F32)

    @pl.when(kk == pl.num_programs(1) - 1)
    def _finalize():
        o_ref[...] = x_ref[...] + _rms(acc_ref[...], g_ref[...])


def _down_proj(a, w, x2d, g, tm=512, tk=512):
    t = a.shape[0]
    return pl.pallas_call(
        _down_kernel,
        grid=(t // tm, D_FF // tk),
        in_specs=[pl.BlockSpec((tm, tk), lambda i, kk: (i, kk)),
                  pl.BlockSpec((tk, D_MODEL), lambda i, kk: (kk, 0)),
                  pl.BlockSpec((tm, D_MODEL), lambda i, kk: (i, 0)),
                  pl.BlockSpec((1, D_MODEL), lambda i, kk: (0, 0))],
        out_specs=pl.BlockSpec((tm, D_MODEL), lambda i, kk: (i, 0)),
        out_shape=jax.ShapeDtypeStruct((t, D_MODEL), F32),
        scratch_shapes=[pltpu.VMEM((tm, D_MODEL), F32)],
        compiler_params=_params(2),
        name="down_proj",
    )(a, w, x2d, g.reshape(1, D_MODEL))


def _rope_tables(s):
    rows = s // GRID_W
    row = jnp.repeat(jnp.arange(rows, dtype=F32), GRID_W)
    col = jnp.tile(jnp.arange(GRID_W, dtype=F32), rows)
    quarter = GQA_HD // 4
    freqs = ROPE_THETA ** (-jnp.arange(quarter, dtype=F32) / quarter)
    ang_r = row[:, None] * freqs[None, :]
    ang_c = col[:, None] * freqs[None, :]
    cos = jnp.concatenate([jnp.cos(ang_r), jnp.cos(ang_r), jnp.cos(ang_c), jnp.cos(ang_c)], axis=-1)
    sin = jnp.concatenate([-jnp.sin(ang_r), jnp.sin(ang_r), -jnp.sin(ang_c), jnp.sin(ang_c)], axis=-1)
    return cos, sin


def _trunk(x, weights):
    b, s, _ = x.shape
    x2d = x.reshape(b * s, D_MODEL)
    cos, sin = _rope_tables(s)
    slopes = jnp.exp2(-8.0 * jnp.arange(1, DIFF_HEADS + 1, dtype=F32) / DIFF_HEADS)
    for l in range(DEPTH):
        (mix_pre_g, w_in, diff_lambda, diff_subln_g, gqa_qnorm_g, gqa_knorm_g, w_proj_a, w_proj_b,
         w_out, mix_post_g, ffn_pre_g, w_gate_up, w_down, ffn_post_g) = [w[l] for w in weights]
        lambda_init = 0.8 - 0.6 * math.exp(-0.3 * l)
        h = _prenorm(x2d, mix_pre_g)
        dqt = _proj_heads(h, w_in, b, s, OFF_DQ, DIFF_QK_WIDTH, transpose=True,
                          scale=DIFF_HD ** -0.5 * LOG2E)
        dk = _proj_heads(h, w_in, b, s, OFF_DK, DIFF_QK_WIDTH, transpose=False)
        dvt = _proj_heads(h, w_in, b, s, OFF_DV, DIFF_WIDTH, transpose=True)
        o_a = _diff_attention(dqt, dk, dvt, slopes, diff_lambda, diff_subln_g, lambda_init)
        gqt = _proj_heads(h, w_in, b, s, OFF_GQ, GQA_WIDTH, transpose=True, scale=GQA_HD ** -0.5 * LOG2E,
                          rope=(gqa_qnorm_g, cos, sin))
        gk = _proj_heads(h, w_in, b, s, OFF_GK, GQA_KV_WIDTH, transpose=False,
                         rope=(gqa_knorm_g, cos, sin))
        gvt = _proj_heads(h, w_in, b, s, OFF_GV, GQA_KV_WIDTH, transpose=True)
        o_b = _gqa_attention(gqt, gk, gvt)
        gate = _proj_gate(h, w_in)
        merged = _merge(o_a.reshape(b * s, DIFF_WIDTH), o_b.reshape(b * s, GQA_WIDTH),
                        w_proj_a, w_proj_b, gate)
        x2d = _out_proj(merged, w_out, x2d, mix_post_g)
        h = _prenorm(x2d, ffn_pre_g)
        act = _swiglu(h, w_gate_up)
        x2d = _down_proj(act, w_down, x2d, ffn_post_g)
    return x2d.reshape(b, s, D_MODEL)


def kernel(x_prompt, x_sample, mix_pre_g, w_in, diff_lambda, diff_subln_g, gqa_qnorm_g, gqa_knorm_g,
           w_proj_a, w_proj_b, w_out, mix_post_g, ffn_pre_g, w_gate_up, w_down, ffn_post_g):
    weights = (mix_pre_g, w_in.astype(BF16), diff_lambda, diff_subln_g, gqa_qnorm_g, gqa_knorm_g,
               w_proj_a.astype(BF16), w_proj_b.astype(BF16), w_out.astype(BF16), mix_post_g,
               ffn_pre_g, w_gate_up.astype(BF16), w_down.astype(BF16), ffn_post_g)
    return (_trunk(x_prompt, weights), _trunk(x_sample, weights))
```

```python
import functools
import math

import jax
import jax.numpy as jnp
from jax import lax
from jax.experimental import pallas as pl
from jax.experimental.pallas import tpu as pltpu

D_MODEL = 2048
DEPTH = 2
GRID_W = 64
EPS = 1e-6
DIFF_HEADS = 8
DIFF_HD = 64
DIFF_V_HD = 128
DIFF_QK_WIDTH = 1024
DIFF_WIDTH = 1024
GQA_HEADS = 8
GQA_KV_HEADS = 2
GQA_GROUP = GQA_HEADS // GQA_KV_HEADS
GQA_HD = 128
GQA_WIDTH = 1024
GQA_KV_WIDTH = 256
ROPE_THETA = 10000.0
GATE_WIDTH = 2 * D_MODEL
D_FF = 5632

HEAD_LANES = 128
OFF_DQ = 0
OFF_DK = OFF_DQ + DIFF_QK_WIDTH
OFF_DV = OFF_DK + DIFF_QK_WIDTH
OFF_GQ = OFF_DV + DIFF_WIDTH
OFF_GK = OFF_GQ + GQA_WIDTH
OFF_GV = OFF_GK + GQA_KV_WIDTH
OFF_GATE = OFF_GV + GQA_KV_WIDTH

VMEM_LIMIT_BYTES = 52 * 1024 * 1024
ATT_BLK = 512
ATT_LOOKAHEAD = 2
LOG2E = math.log2(math.e)

BF16 = jnp.bfloat16
F32 = jnp.float32


def _params(n_grid):
    return pltpu.CompilerParams(
        dimension_semantics=("parallel",) * (n_grid - 1) + ("arbitrary",),
        vmem_limit_bytes=VMEM_LIMIT_BYTES)


def _rms(x, g):
    ms = jnp.mean(x * x, axis=-1, keepdims=True)
    return x * lax.rsqrt(ms + EPS) * g


def _prenorm_kernel(x_ref, g_ref, o_ref):
    o_ref[...] = _rms(x_ref[...], g_ref[...]).astype(o_ref.dtype)


def _prenorm(x2d, g, tm=1024):
    t = x2d.shape[0]
    return pl.pallas_call(
        _prenorm_kernel,
        grid=(t // tm,),
        in_specs=[pl.BlockSpec((tm, D_MODEL), lambda i: (i, 0)),
                  pl.BlockSpec((1, D_MODEL), lambda i: (0, 0))],
        out_specs=pl.BlockSpec((tm, D_MODEL), lambda i: (i, 0)),
        out_shape=jax.ShapeDtypeStruct((t, D_MODEL), BF16),
        compiler_params=_params(1),
        name="prenorm",
    )(x2d, g.reshape(1, D_MODEL))


def _rope(y, cos, sin_signed):
    lane = lax.broadcasted_iota(jnp.int32, y.shape, 1)
    up = pltpu.roll(y, 96, 1)
    dn = pltpu.roll(y, 32, 1)
    partner = jnp.where((lane % 64) < 32, up, dn)
    return y * cos + partner * sin_signed


def _store_head(o_ref, hh, y, transpose):
    if transpose:
        for c in range(y.shape[0] // ATT_BLK):
            o_ref[0, hh, c] = y[c * ATT_BLK:(c + 1) * ATT_BLK].T.astype(o_ref.dtype)
    else:
        o_ref[0, hh] = y.astype(o_ref.dtype)


def _seg_heads_kernel(h_ref, w_ref, o_ref, *, scale, transpose, n_heads):
    acc = jnp.dot(h_ref[...], w_ref[...], preferred_element_type=F32)
    for hh in range(n_heads):
        y = acc[:, hh * HEAD_LANES:(hh + 1) * HEAD_LANES]
        if scale != 1.0:
            y = y * scale
        _store_head(o_ref, hh, y, transpose)


def _seg_rope_kernel(h_ref, w_ref, g_ref, cos_ref, sin_ref, o_ref, *, scale, transpose, n_heads):
    acc = jnp.dot(h_ref[...], w_ref[...], preferred_element_type=F32)
    cos = cos_ref[...]
    sin = sin_ref[...]
    g = g_ref[...]
    for hh in range(n_heads):
        y = _rms(acc[:, hh * HEAD_LANES:(hh + 1) * HEAD_LANES], g)
        y = _rope(y, cos, sin)
        if scale != 1.0:
            y = y * scale
        _store_head(o_ref, hh, y, transpose)


def _seg_gate_kernel(h_ref, w_ref, o_ref):
    acc = jnp.dot(h_ref[...], w_ref[...], preferred_element_type=F32)
    o_ref[...] = jax.nn.sigmoid(acc).astype(o_ref.dtype)


def _proj_heads(h2d, w_in, b, s, col0, ncols, *, transpose, scale=1.0, rope=None, tm=1024, tn=512):
    tn = min(tn, ncols)
    n_heads_blk = tn // HEAD_LANES
    n_heads = ncols // HEAD_LANES
    nsb = s // tm
    grid = (b * nsb, ncols // tn)
    cb0 = col0 // tn
    assert col0 % tn == 0 and s % tm == 0
    in_specs = [pl.BlockSpec((tm, D_MODEL), lambda i, j: (i, 0)),
                pl.BlockSpec((D_MODEL, tn), lambda i, j: (0, cb0 + j))]
    args = [h2d, w_in]
    if rope is None:
        body = functools.partial(_seg_heads_kernel, scale=scale, transpose=transpose, n_heads=n_heads_blk)
    else:
        gain, cos, sin = rope
        in_specs += [pl.BlockSpec((1, HEAD_LANES), lambda i, j: (0, 0)),
                     pl.BlockSpec((tm, HEAD_LANES), lambda i, j: (i % nsb, 0)),
                     pl.BlockSpec((tm, HEAD_LANES), lambda i, j: (i % nsb, 0))]
        args += [gain.reshape(1, HEAD_LANES), cos, sin]
        body = functools.partial(_seg_rope_kernel, scale=scale, transpose=transpose, n_heads=n_heads_blk)
    if transpose:
        out_shape = (b, n_heads, s // ATT_BLK, HEAD_LANES, ATT_BLK)
        out_spec = pl.BlockSpec((1, n_heads_blk, tm // ATT_BLK, HEAD_LANES, ATT_BLK),
                                lambda i, j: (i // nsb, j, i % nsb, 0, 0))
    else:
        out_shape = (b, n_heads, s, HEAD_LANES)
        out_spec = pl.BlockSpec((1, n_heads_blk, tm, HEAD_LANES), lambda i, j: (i // nsb, j, i % nsb, 0))
    return pl.pallas_call(
        body, grid=grid, in_specs=in_specs, out_specs=out_spec,
        out_shape=jax.ShapeDtypeStruct(out_shape, BF16),
        compiler_params=_params(2),
        name="proj_heads",
    )(*args)


def _proj_gate(h2d, w_in, tm=1024, tn=512):
    t = h2d.shape[0]
    cb0 = OFF_GATE // tn
    return pl.pallas_call(
        _seg_gate_kernel,
        grid=(t // tm, GATE_WIDTH // tn),
        in_specs=[pl.BlockSpec((tm, D_MODEL), lambda i, j: (i, 0)),
                  pl.BlockSpec((D_MODEL, tn), lambda i, j: (0, cb0 + j))],
        out_specs=pl.BlockSpec((tm, tn), lambda i, j: (i, j)),
        out_shape=jax.ShapeDtypeStruct((t, GATE_WIDTH), F32),
        compiler_params=_params(2),
        name="proj_gate",
    )(h2d, w_in)


BOUNDED_STABILISER_MAX = 32.0
NORM_MARGIN = 1.02
EXP2_UNDERFLOW = 127.0


def _online_softmax_step(s, c, vt, m_ref, l_ref, acc_ref, idx):
    m_old = m_ref[idx]
    m_new = jnp.maximum(m_old, jnp.max(s, axis=0, keepdims=True) + c)
    p = jnp.exp2(s - (m_new - c))
    alpha = jnp.exp2(m_old - m_new)
    l_ref[idx] = alpha * l_ref[idx] + jnp.sum(p, axis=0, keepdims=True)
    acc_ref[idx] = alpha * acc_ref[idx] + jnp.dot(vt, p.astype(BF16), preferred_element_type=F32)
    m_ref[idx] = m_new


def _bounded_softmax_step(t, vt, l_ref, acc_ref, idx):
    p = jnp.exp2(t)
    l_ref[idx] = l_ref[idx] + jnp.sum(p, axis=0, keepdims=True)
    acc_ref[idx] = acc_ref[idx] + jnp.dot(vt, p.astype(BF16), preferred_element_type=F32)


def _sweep_key_blocks(lo, hi, keys, values_t, score, step, n_chains, s_s, unroll=1):
    ahead = ATT_LOOKAHEAD
    k_first = keys(lo)
    for i in range(ahead):
        s_s[i] = score(k_first, i)

    def key_block(j, carry):
        k = keys(j)
        k_next = keys(jnp.minimum(j + 1, hi - 1))
        vt = values_t(j)
        queue = [s_s[i] for i in range(ahead)]
        for ci in range(n_chains):
            nxt = ci + ahead
            queue.append(score(k, nxt) if nxt < n_chains else score(k_next, nxt - n_chains))
            step(queue.pop(0), ci, j, vt)
        for i in range(ahead):
            s_s[i] = queue[i]
        return carry

    lax.fori_loop(lo, hi, key_block, 0, unroll=unroll)


def _column_norms(x_t):
    x = x_t.astype(F32)
    return jnp.sqrt(jnp.sum(x * x, axis=0, keepdims=True))


def _diff_attn_kernel(slopes_ref, windows_ref, lam_ref, g_ref, qt_ref, k_ref, vt_ref, o_ref,
                      q_s, bias_s, s_s, u_s, m_s, l_s, acc_s, kmax_s, *, lambda_init, n_kblk):
    blk = ATT_BLK
    h = pl.program_id(1)
    qi = pl.program_id(2)
    slope = slopes_ref[h] * LOG2E
    half = DIFF_HD

    def keys(j):
        return k_ref[0, 0, pl.ds(pl.multiple_of(j * blk, blk), blk), :]

    @pl.when(qi == 0)
    def _head_init():
        rel = (lax.broadcasted_iota(jnp.int32, (blk, blk), 0)
               - lax.broadcasted_iota(jnp.int32, (blk, blk), 1)).astype(F32)
        bias_s[0] = slope * rel
        bias_s[1] = -slope * jnp.abs(rel)
        bias_s[2] = -slope * rel

        first_map = lax.broadcasted_iota(jnp.int32, (blk, HEAD_LANES), 1) < half

        def norm_block(j, best):
            kb = keys(j).astype(F32)
            sq = kb * kb
            n0 = jnp.sqrt(jnp.sum(jnp.where(first_map, sq, 0.0), axis=1, keepdims=True))
            n1 = jnp.sqrt(jnp.sum(jnp.where(first_map, 0.0, sq), axis=1, keepdims=True))
            return jnp.maximum(best[0], jnp.max(n0)), jnp.maximum(best[1], jnp.max(n1))

        kmax = lax.fori_loop(0, n_kblk, norm_block, (jnp.float32(0.0), jnp.float32(0.0)))
        kmax_s[0] = kmax[0]
        kmax_s[1] = kmax[1]

    zeros = jnp.zeros((half, blk), BF16)
    q_s[0, 0:half, :] = qt_ref[0, 0, 0, 0:half, :]
    q_s[0, half:, :] = zeros
    q_s[1, 0:half, :] = zeros
    q_s[1, half:, :] = qt_ref[0, 0, 0, half:, :]
    m_s[...] = jnp.full(m_s.shape, -jnp.inf, F32)
    l_s[...] = jnp.zeros(l_s.shape, F32)
    acc_s[...] = jnp.zeros(acc_s.shape, F32)

    u_max = jnp.float32(0.0)
    for mp in range(2):
        u = _column_norms(q_s[mp]) * (kmax_s[mp] * NORM_MARGIN)
        u_s[mp] = u
        u_max = jnp.maximum(u_max, jnp.max(u))
    bounded = u_max <= BOUNDED_STABILISER_MAX

    def score(kblk, mp):
        return jnp.dot(kblk, q_s[mp], preferred_element_type=F32)

    def tile_bias(j):
        side = jnp.where(j < qi, 0, jnp.where(j == qi, 1, 2))
        return bias_s[side], -slope * (jnp.abs(j - qi) * blk).astype(F32)

    @pl.when(bounded)
    def _bounded_sweep():
        w = windows_ref[h]
        lo = jnp.maximum(qi - w, 0)
        hi = jnp.minimum(qi + w + 1, n_kblk)

        def step(s, mp, j, vt):
            bias, c = tile_bias(j)
            _bounded_softmax_step(s + bias + (c - u_s[mp]), vt, l_s, acc_s, mp)

        _sweep_key_blocks(lo, hi, keys, lambda j: vt_ref[0, 0, j], score, step, 2, s_s)

    @pl.when(jnp.logical_not(bounded))
    def _online_sweep():
        def step(s, mp, j, vt):
            bias, c = tile_bias(j)
            _online_softmax_step(s + bias, c, vt, m_s, l_s, acc_s, mp)

        _sweep_key_blocks(0, n_kblk, keys, lambda j: vt_ref[0, 0, j], score, step, 2, s_s)

    dl = lam_ref[...]
    lam = (jnp.exp(jnp.sum(dl[0:1] * dl[1:2], axis=-1, keepdims=True))
           - jnp.exp(jnp.sum(dl[2:3] * dl[3:4], axis=-1, keepdims=True)) + lambda_init)
    o = acc_s[0] * (1.0 / l_s[0]) - lam * (acc_s[1] * (1.0 / l_s[1]))
    ms = jnp.mean(o * o, axis=0, keepdims=True)
    y = o * lax.rsqrt(ms + EPS) * g_ref[...] * (1.0 - lambda_init)
    o_ref[0] = y.T.astype(o_ref.dtype)


def _alibi_windows(slopes, blk):
    dist = EXP2_UNDERFLOW / (slopes * LOG2E)
    return (jnp.floor((dist - 1.0) / blk) + 1.0).astype(jnp.int32)


def _diff_attention(qt, k, vt, slopes, diff_lambda, subln_g, lambda_init):
    b, nh, nb, _, blk = qt.shape
    s = nb * blk
    body = functools.partial(_diff_attn_kernel, lambda_init=lambda_init, n_kblk=nb)
    return pl.pallas_call(
        body,
        grid=(b, nh, nb),
        in_specs=[pl.BlockSpec(memory_space=pltpu.SMEM),
                  pl.BlockSpec(memory_space=pltpu.SMEM),
                  pl.BlockSpec((4, DIFF_HD), lambda bb, hh, qi: (0, 0)),
                  pl.BlockSpec((DIFF_V_HD, 1), lambda bb, hh, qi: (0, 0)),
                  pl.BlockSpec((1, 1, 1, HEAD_LANES, blk), lambda bb, hh, qi: (bb, hh, qi, 0, 0)),
                  pl.BlockSpec((1, 1, s, HEAD_LANES), lambda bb, hh, qi: (bb, hh, 0, 0)),
                  pl.BlockSpec((1, 1, nb, HEAD_LANES, blk), lambda bb, hh, qi: (bb, hh, 0, 0, 0))],
        out_specs=pl.BlockSpec((1, blk, HEAD_LANES), lambda bb, hh, qi: (bb, qi, hh)),
        out_shape=jax.ShapeDtypeStruct((b, s, nh * HEAD_LANES), BF16),
        scratch_shapes=[pltpu.VMEM((2, HEAD_LANES, blk), BF16),
                        pltpu.VMEM((3, blk, blk), F32),
                        pltpu.VMEM((ATT_LOOKAHEAD, blk, blk), F32),
                        pltpu.VMEM((2, 1, blk), F32),
                        pltpu.VMEM((2, 1, blk), F32),
                        pltpu.VMEM((2, 1, blk), F32),
                        pltpu.VMEM((2, HEAD_LANES, blk), F32),
                        pltpu.SMEM((2,), F32)],
        compiler_params=pltpu.CompilerParams(dimension_semantics=("arbitrary",) * 3,
                                             vmem_limit_bytes=VMEM_LIMIT_BYTES),
        name="diff_attention",
    )(slopes, _alibi_windows(slopes, blk), diff_lambda, subln_g.reshape(DIFF_V_HD, 1), qt, k, vt)


def _gqa_attn_kernel(qt_ref, k_ref, vt_ref, o_ref, s_s, u_s, m_s, l_s, acc_s, kmax_s, *, n_kblk):
    blk = ATT_BLK
    qi = pl.program_id(2)

    def keys(j):
        return k_ref[0, 0, pl.ds(pl.multiple_of(j * blk, blk), blk), :]

    @pl.when(qi == 0)
    def _kv_head_init():
        def norm_block(j, best):
            kb = keys(j).astype(F32)
            return jnp.maximum(best, jnp.max(jnp.sqrt(jnp.sum(kb * kb, axis=1, keepdims=True))))

        kmax_s[0] = lax.fori_loop(0, n_kblk, norm_block, jnp.float32(0.0))

    m_s[...] = jnp.full(m_s.shape, -jnp.inf, F32)
    l_s[...] = jnp.zeros(l_s.shape, F32)
    acc_s[...] = jnp.zeros(acc_s.shape, F32)

    u_max = jnp.float32(0.0)
    for g in range(GQA_GROUP):
        u = _column_norms(qt_ref[0, g, 0]) * (kmax_s[0] * NORM_MARGIN)
        u_s[g] = u
        u_max = jnp.maximum(u_max, jnp.max(u))
    bounded = u_max <= BOUNDED_STABILISER_MAX

    def score(kblk, g):
        return jnp.dot(kblk, qt_ref[0, g, 0], preferred_element_type=F32)

    @pl.when(bounded)
    def _bounded_sweep():
        def step(s, g, j, vt):
            _bounded_softmax_step(s - u_s[g], vt, l_s, acc_s, g)

        _sweep_key_blocks(0, n_kblk, keys, lambda j: vt_ref[0, 0, j], score, step, GQA_GROUP, s_s, unroll=2)

    @pl.when(jnp.logical_not(bounded))
    def _online_sweep():
        def step(s, g, j, vt):
            _online_softmax_step(s, 0.0, vt, m_s, l_s, acc_s, g)

        _sweep_key_blocks(0, n_kblk, keys, lambda j: vt_ref[0, 0, j], score, step, GQA_GROUP, s_s)

    for g in range(GQA_GROUP):
        o = acc_s[g] * (1.0 / l_s[g])
        o_ref[0, :, g * HEAD_LANES:(g + 1) * HEAD_LANES] = o.T.astype(o_ref.dtype)


def _gqa_attention(qt, k, vt):
    b, nh, nb, _, blk = qt.shape
    s = nb * blk
    nkv = nh // GQA_GROUP
    body = functools.partial(_gqa_attn_kernel, n_kblk=nb)
    return pl.pallas_call(
        body,
        grid=(b, nkv, nb),
        in_specs=[pl.BlockSpec((1, GQA_GROUP, 1, HEAD_LANES, blk), lambda bb, kv, qi: (bb, kv, qi, 0, 0)),
                  pl.BlockSpec((1, 1, s, HEAD_LANES), lambda bb, kv, qi: (bb, kv, 0, 0)),
                  pl.BlockSpec((1, 1, nb, HEAD_LANES, blk), lambda bb, kv, qi: (bb, kv, 0, 0, 0))],
        out_specs=pl.BlockSpec((1, blk, GQA_GROUP * HEAD_LANES), lambda bb, kv, qi: (bb, qi, kv)),
        out_shape=jax.ShapeDtypeStruct((b, s, nh * HEAD_LANES), BF16),
        scratch_shapes=[pltpu.VMEM((ATT_LOOKAHEAD, blk, blk), F32),
                        pltpu.VMEM((GQA_GROUP, 1, blk), F32),
                        pltpu.VMEM((GQA_GROUP, 1, blk), F32),
                        pltpu.VMEM((GQA_GROUP, 1, blk), F32),
                        pltpu.VMEM((GQA_GROUP, HEAD_LANES, blk), F32),
                        pltpu.SMEM((1,), F32)],
        compiler_params=pltpu.CompilerParams(dimension_semantics=("arbitrary",) * 3,
                                             vmem_limit_bytes=VMEM_LIMIT_BYTES),
        name="gqa_attention",
    )(qt, k, vt)


def _merge_kernel(oa_ref, ob_ref, wa_ref, wb_ref, ga_ref, gb_ref, o_ref):
    a = jnp.dot(oa_ref[...], wa_ref[...], preferred_element_type=F32)
    bb = jnp.dot(ob_ref[...], wb_ref[...], preferred_element_type=F32)
    o_ref[...] = (ga_ref[...] * a + gb_ref[...] * bb).astype(o_ref.dtype)


def _merge(oa, ob, wa, wb, gate, tm=1024, tn=512):
    t = oa.shape[0]
    gb0 = D_MODEL // tn
    return pl.pallas_call(
        _merge_kernel,
        grid=(t // tm, D_MODEL // tn),
        in_specs=[pl.BlockSpec((tm, DIFF_WIDTH), lambda i, j: (i, 0)),
                  pl.BlockSpec((tm, GQA_WIDTH), lambda i, j: (i, 0)),
                  pl.BlockSpec((DIFF_WIDTH, tn), lambda i, j: (0, j)),
                  pl.BlockSpec((GQA_WIDTH, tn), lambda i, j: (0, j)),
                  pl.BlockSpec((tm, tn), lambda i, j: (i, j)),
                  pl.BlockSpec((tm, tn), lambda i, j: (i, gb0 + j))],
        out_specs=pl.BlockSpec((tm, tn), lambda i, j: (i, j)),
        out_shape=jax.ShapeDtypeStruct((t, D_MODEL), BF16),
        compiler_params=_params(2),
        name="merge",
    )(oa, ob, wa, wb, gate, gate)


def _out_proj_kernel(a_ref, w_ref, x_ref, g_ref, o_ref):
    m = jnp.dot(a_ref[...], w_ref[...], preferred_element_type=F32)
    o_ref[...] = x_ref[...] + _rms(m, g_ref[...])


def _out_proj(a, w, x2d, g, tm=512):
    t = a.shape[0]
    return pl.pallas_call(
        _out_proj_kernel,
        grid=(t // tm,),
        in_specs=[pl.BlockSpec((tm, D_MODEL), lambda i: (i, 0)),
                  pl.BlockSpec((D_MODEL, D_MODEL), lambda i: (0, 0)),
                  pl.BlockSpec((tm, D_MODEL), lambda i: (i, 0)),
                  pl.BlockSpec((1, D_MODEL), lambda i: (0, 0))],
        out_specs=pl.BlockSpec((tm, D_MODEL), lambda i: (i, 0)),
        out_shape=jax.ShapeDtypeStruct((t, D_MODEL), F32),
        compiler_params=_params(1),
        name="out_proj",
    )(a, w, x2d, g.reshape(1, D_MODEL))


def _swiglu_kernel(h_ref, wg_ref, wu_ref, o_ref):
    h = h_ref[...]
    gte = jnp.dot(h, wg_ref[...], preferred_element_type=F32)
    up = jnp.dot(h, wu_ref[...], preferred_element_type=F32)
    o_ref[...] = (jax.nn.silu(gte) * up).astype(o_ref.dtype)


def _swiglu(h2d, w_gate_up, tm=1024, tn=512):
    t = h2d.shape[0]
    ub0 = D_FF // tn
    return pl.pallas_call(
        _swiglu_kernel,
        grid=(t // tm, D_FF // tn),
        in_specs=[pl.BlockSpec((tm, D_MODEL), lambda i, j: (i, 0)),
                  pl.BlockSpec((D_MODEL, tn), lambda i, j: (0, j)),
                  pl.BlockSpec((D_MODEL, tn), lambda i, j: (0, ub0 + j))],
        out_specs=pl.BlockSpec((tm, tn), lambda i, j: (i, j)),
        out_shape=jax.ShapeDtypeStruct((t, D_FF), BF16),
        compiler_params=_params(2),
        name="swiglu",
    )(h2d, w_gate_up, w_gate_up)


def _down_kernel(a_ref, w_ref, x_ref, g_ref, o_ref, acc_ref):
    kk = pl.program_id(1)

    @pl.when(kk == 0)
    def _init():
        acc_ref[...] = jnp.zeros(acc_ref.shape, F32)

    acc_ref[...] += jnp.dot(a_ref[...], w_ref[...], preferred_element_type=F32)

    @pl.when(kk == pl.num_programs(1) - 1)
    def _finalize():
        o_ref[...] = x_ref[...] + _rms(acc_ref[...], g_ref[...])


def _down_proj(a, w, x2d, g, tm=512, tk=512):
    t = a.shape[0]
    return pl.pallas_call(
        _down_kernel,
        grid=(t // tm, D_FF // tk),
        in_specs=[pl.BlockSpec((tm, tk), lambda i, kk: (i, kk)),
                  pl.BlockSpec((tk, D_MODEL), lambda i, kk: (kk, 0)),
                  pl.BlockSpec((tm, D_MODEL), lambda i, kk: (i, 0)),
                  pl.BlockSpec((1, D_MODEL), lambda i, kk: (0, 0))],
        out_specs=pl.BlockSpec((tm, D_MODEL), lambda i, kk: (i, 0)),
        out_shape=jax.ShapeDtypeStruct((t, D_MODEL), F32),
        scratch_shapes=[pltpu.VMEM((tm, D_MODEL), F32)],
        compiler_params=_params(2),
        name="down_proj",
    )(a, w, x2d, g.reshape(1, D_MODEL))


def _rope_tables(s):
    rows = s // GRID_W
    row = jnp.repeat(jnp.arange(rows, dtype=F32), GRID_W)
    col = jnp.tile(jnp.arange(GRID_W, dtype=F32), rows)
    quarter = GQA_HD // 4
    freqs = ROPE_THETA ** (-jnp.arange(quarter, dtype=F32) / quarter)
    ang_r = row[:, None] * freqs[None, :]
    ang_c = col[:, None] * freqs[None, :]
    cos = jnp.concatenate([jnp.cos(ang_r), jnp.cos(ang_r), jnp.cos(ang_c), jnp.cos(ang_c)], axis=-1)
    sin = jnp.concatenate([-jnp.sin(ang_r), jnp.sin(ang_r), -jnp.sin(ang_c), jnp.sin(ang_c)], axis=-1)
    return cos, sin


def _trunk(x, weights):
    b, s, _ = x.shape
    x2d = x.reshape(b * s, D_MODEL)
    cos, sin = _rope_tables(s)
    slopes = jnp.exp2(-8.0 * jnp.arange(1, DIFF_HEADS + 1, dtype=F32) / DIFF_HEADS)
    for l in range(DEPTH):
        (mix_pre_g, w_in, diff_lambda, diff_subln_g, gqa_qnorm_g, gqa_knorm_g, w_proj_a, w_proj_b,
         w_out, mix_post_g, ffn_pre_g, w_gate_up, w_down, ffn_post_g) = [w[l] for w in weights]
        lambda_init = 0.8 - 0.6 * math.exp(-0.3 * l)
        h = _prenorm(x2d, mix_pre_g)
        dqt = _proj_heads(h, w_in, b, s, OFF_DQ, DIFF_QK_WIDTH, transpose=True,
                          scale=DIFF_HD ** -0.5 * LOG2E)
        dk = _proj_heads(h, w_in, b, s, OFF_DK, DIFF_QK_WIDTH, transpose=False)
        dvt = _proj_heads(h, w_in, b, s, OFF_DV, DIFF_WIDTH, transpose=True)
        o_a = _diff_attention(dqt, dk, dvt, slopes, diff_lambda, diff_subln_g, lambda_init)
        gqt = _proj_heads(h, w_in, b, s, OFF_GQ, GQA_WIDTH, transpose=True, scale=GQA_HD ** -0.5 * LOG2E,
                          rope=(gqa_qnorm_g, cos, sin))
        gk = _proj_heads(h, w_in, b, s, OFF_GK, GQA_KV_WIDTH, transpose=False,
                         rope=(gqa_knorm_g, cos, sin))
        gvt = _proj_heads(h, w_in, b, s, OFF_GV, GQA_KV_WIDTH, transpose=True)
        o_b = _gqa_attention(gqt, gk, gvt)
        gate = _proj_gate(h, w_in)
        merged = _merge(o_a.reshape(b * s, DIFF_WIDTH), o_b.reshape(b * s, GQA_WIDTH),
                        w_proj_a, w_proj_b, gate)
        x2d = _out_proj(merged, w_out, x2d, mix_post_g)
        h = _prenorm(x2d, ffn_pre_g)
        act = _swiglu(h, w_gate_up)
        x2d = _down_proj(act, w_down, x2d, ffn_post_g)
    return x2d.reshape(b, s, D_MODEL)


def kernel(x_prompt, x_sample, mix_pre_g, w_in, diff_lambda, diff_subln_g, gqa_qnorm_g, gqa_knorm_g,
           w_proj_a, w_proj_b, w_out, mix_post_g, ffn_pre_g, w_gate_up, w_down, ffn_post_g):
    weights = (mix_pre_g, w_in.astype(BF16), diff_lambda, diff_subln_g, gqa_qnorm_g, gqa_knorm_g,
               w_proj_a.astype(BF16), w_proj_b.astype(BF16), w_out.astype(BF16), mix_post_g,
               ffn_pre_g, w_gate_up.astype(BF16), w_down.astype(BF16), ffn_post_g)
    return (_trunk(x_prompt, weights), _trunk(x_sample, weights))
```

```python
import functools
import math

import jax
import jax.numpy as jnp
from jax import lax
from jax.experimental import pallas as pl
from jax.experimental.pallas import tpu as pltpu

D_MODEL = 2048
DEPTH = 2
GRID_W = 64
EPS = 1e-6
DIFF_HEADS = 8
DIFF_HD = 64
DIFF_V_HD = 128
DIFF_QK_WIDTH = 1024
DIFF_WIDTH = 1024
GQA_HEADS = 8
GQA_KV_HEADS = 2
GQA_GROUP = GQA_HEADS // GQA_KV_HEADS
GQA_HD = 128
GQA_WIDTH = 1024
GQA_KV_WIDTH = 256
ROPE_THETA = 10000.0
GATE_WIDTH = 2 * D_MODEL
D_FF = 5632

HEAD_LANES = 128
OFF_DQ = 0
OFF_DK = OFF_DQ + DIFF_QK_WIDTH
OFF_DV = OFF_DK + DIFF_QK_WIDTH
OFF_GQ = OFF_DV + DIFF_WIDTH
OFF_GK = OFF_GQ + GQA_WIDTH
OFF_GV = OFF_GK + GQA_KV_WIDTH
OFF_GATE = OFF_GV + GQA_KV_WIDTH

VMEM_LIMIT_BYTES = 52 * 1024 * 1024
ATT_BLK = 512
DIFF_LOOKAHEAD = 2
GQA_LOOKAHEAD = 1
LOG2E = math.log2(math.e)

BF16 = jnp.bfloat16
F32 = jnp.float32


def _params(n_grid):
    return pltpu.CompilerParams(
        dimension_semantics=("parallel",) * (n_grid - 1) + ("arbitrary",),
        vmem_limit_bytes=VMEM_LIMIT_BYTES)


def _rms(x, g):
    ms = jnp.mean(x * x, axis=-1, keepdims=True)
    return x * lax.rsqrt(ms + EPS) * g


def _prenorm_kernel(x_ref, g_ref, o_ref):
    o_ref[...] = _rms(x_ref[...], g_ref[...]).astype(o_ref.dtype)


def _prenorm(x2d, g, tm=1024):
    t = x2d.shape[0]
    return pl.pallas_call(
        _prenorm_kernel,
        grid=(t // tm,),
        in_specs=[pl.BlockSpec((tm, D_MODEL), lambda i: (i, 0)),
                  pl.BlockSpec((1, D_MODEL), lambda i: (0, 0))],
        out_specs=pl.BlockSpec((tm, D_MODEL), lambda i: (i, 0)),
        out_shape=jax.ShapeDtypeStruct((t, D_MODEL), BF16),
        compiler_params=_params(1),
        name="prenorm",
    )(x2d, g.reshape(1, D_MODEL))


def _rope(y, cos, sin_signed):
    lane = lax.broadcasted_iota(jnp.int32, y.shape, 1)
    up = pltpu.roll(y, 96, 1)
    dn = pltpu.roll(y, 32, 1)
    partner = jnp.where((lane % 64) < 32, up, dn)
    return y * cos + partner * sin_signed


def _store_head(o_ref, hh, y, transpose):
    if transpose:
        for c in range(y.shape[0] // ATT_BLK):
            o_ref[0, hh, c] = y[c * ATT_BLK:(c + 1) * ATT_BLK].T.astype(o_ref.dtype)
    else:
        o_ref[0, hh] = y.astype(o_ref.dtype)


def _seg_heads_kernel(h_ref, w_ref, o_ref, *, scale, transpose, n_heads):
    acc = jnp.dot(h_ref[...], w_ref[...], preferred_element_type=F32)
    for hh in range(n_heads):
        y = acc[:, hh * HEAD_LANES:(hh + 1) * HEAD_LANES]
        if scale != 1.0:
            y = y * scale
        _store_head(o_ref, hh, y, transpose)


def _seg_rope_kernel(h_ref, w_ref, g_ref, cos_ref, sin_ref, o_ref, *, scale, transpose, n_heads):
    acc = jnp.dot(h_ref[...], w_ref[...], preferred_element_type=F32)
    cos = cos_ref[...]
    sin = sin_ref[...]
    g = g_ref[...]
    for hh in range(n_heads):
        y = _rms(acc[:, hh * HEAD_LANES:(hh + 1) * HEAD_LANES], g)
        y = _rope(y, cos, sin)
        if scale != 1.0:
            y = y * scale
        _store_head(o_ref, hh, y, transpose)


def _seg_gate_kernel(h_ref, w_ref, o_ref):
    acc = jnp.dot(h_ref[...], w_ref[...], preferred_element_type=F32)
    o_ref[...] = jax.nn.sigmoid(acc).astype(o_ref.dtype)


def _proj_heads(h2d, w_in, b, s, col0, ncols, *, transpose, scale=1.0, rope=None, tm=1024, tn=512):
    tn = min(tn, ncols)
    n_heads_blk = tn // HEAD_LANES
    n_heads = ncols // HEAD_LANES
    nsb = s // tm
    grid = (b * nsb, ncols // tn)
    cb0 = col0 // tn
    assert col0 % tn == 0 and s % tm == 0
    in_specs = [pl.BlockSpec((tm, D_MODEL), lambda i, j: (i, 0)),
                pl.BlockSpec((D_MODEL, tn), lambda i, j: (0, cb0 + j))]
    args = [h2d, w_in]
    if rope is None:
        body = functools.partial(_seg_heads_kernel, scale=scale, transpose=transpose, n_heads=n_heads_blk)
    else:
        gain, cos, sin = rope
        in_specs += [pl.BlockSpec((1, HEAD_LANES), lambda i, j: (0, 0)),
                     pl.BlockSpec((tm, HEAD_LANES), lambda i, j: (i % nsb, 0)),
                     pl.BlockSpec((tm, HEAD_LANES), lambda i, j: (i % nsb, 0))]
        args += [gain.reshape(1, HEAD_LANES), cos, sin]
        body = functools.partial(_seg_rope_kernel, scale=scale, transpose=transpose, n_heads=n_heads_blk)
    if transpose:
        out_shape = (b, n_heads, s // ATT_BLK, HEAD_LANES, ATT_BLK)
        out_spec = pl.BlockSpec((1, n_heads_blk, tm // ATT_BLK, HEAD_LANES, ATT_BLK),
                                lambda i, j: (i // nsb, j, i % nsb, 0, 0))
    else:
        out_shape = (b, n_heads, s, HEAD_LANES)
        out_spec = pl.BlockSpec((1, n_heads_blk, tm, HEAD_LANES), lambda i, j: (i // nsb, j, i % nsb, 0))
    return pl.pallas_call(
        body, grid=grid, in_specs=in_specs, out_specs=out_spec,
        out_shape=jax.ShapeDtypeStruct(out_shape, BF16),
        compiler_params=_params(2),
        name="proj_heads",
    )(*args)


def _proj_gate(h2d, w_in, tm=1024, tn=512):
    t = h2d.shape[0]
    cb0 = OFF_GATE // tn
    return pl.pallas_call(
        _seg_gate_kernel,
        grid=(t // tm, GATE_WIDTH // tn),
        in_specs=[pl.BlockSpec((tm, D_MODEL), lambda i, j: (i, 0)),
                  pl.BlockSpec((D_MODEL, tn), lambda i, j: (0, cb0 + j))],
        out_specs=pl.BlockSpec((tm, tn), lambda i, j: (i, j)),
        out_shape=jax.ShapeDtypeStruct((t, GATE_WIDTH), F32),
        compiler_params=_params(2),
        name="proj_gate",
    )(h2d, w_in)


BOUNDED_STABILISER_MAX = 32.0
NORM_MARGIN = 1.02
EXP2_UNDERFLOW = 127.0


def _online_softmax_step(s, c, vt, m_ref, l_ref, acc_ref, idx):
    m_old = m_ref[idx]
    m_new = jnp.maximum(m_old, jnp.max(s, axis=0, keepdims=True) + c)
    p = jnp.exp2(s - (m_new - c))
    alpha = jnp.exp2(m_old - m_new)
    l_ref[idx] = alpha * l_ref[idx] + jnp.sum(p, axis=0, keepdims=True)
    acc_ref[idx] = alpha * acc_ref[idx] + jnp.dot(vt, p.astype(BF16), preferred_element_type=F32)
    m_ref[idx] = m_new


def _bounded_softmax_step(t, vt, l_ref, acc_ref, idx):
    p = jnp.exp2(t)
    l_ref[idx] = l_ref[idx] + jnp.sum(p, axis=0, keepdims=True)
    acc_ref[idx] = acc_ref[idx] + jnp.dot(vt, p.astype(BF16), preferred_element_type=F32)


def _sweep_key_blocks(lo, hi, keys, values_t, score, step, n_chains, s_s, unroll=1):
    ahead = s_s.shape[0]
    k_first = keys(lo)
    for i in range(ahead):
        s_s[i] = score(k_first, i)

    def key_blocks(j0, count):
        queue = [s_s[i] for i in range(ahead)]
        for t in range(count):
            j = j0 + t
            k = keys(j)
            k_next = keys(jnp.minimum(j + 1, hi - 1))
            vt = values_t(j)
            for ci in range(n_chains):
                nxt = ci + ahead
                queue.append(score(k, nxt) if nxt < n_chains else score(k_next, nxt - n_chains))
                step(queue.pop(0), ci, j, vt)
        for i in range(ahead):
            s_s[i] = queue[i]

    n_groups = (hi - lo) // unroll

    def group(i, carry):
        key_blocks(lo + i * unroll, unroll)
        return carry

    lax.fori_loop(0, n_groups, group, 0)
    if unroll > 1 and not (isinstance(n_groups, int) and n_groups * unroll == hi - lo):
        def single(j, carry):
            key_blocks(j, 1)
            return carry

        lax.fori_loop(lo + n_groups * unroll, hi, single, 0)


def _column_norms(x_t):
    x = x_t.astype(F32)
    return jnp.sqrt(jnp.sum(x * x, axis=0, keepdims=True))


def _diff_attn_kernel(slopes_ref, windows_ref, lam_ref, g_ref, qt_ref, k_ref, vt_ref, o_ref,
                      q_s, bias_s, s_s, u_s, m_s, l_s, acc_s, kmax_s, *, lambda_init, n_kblk):
    blk = ATT_BLK
    h = pl.program_id(1)
    qi = pl.program_id(2)
    slope = slopes_ref[h] * LOG2E
    half = DIFF_HD

    def keys(j):
        return k_ref[0, 0, pl.ds(pl.multiple_of(j * blk, blk), blk), :]

    @pl.when(qi == 0)
    def _head_init():
        rel = (lax.broadcasted_iota(jnp.int32, (blk, blk), 0)
               - lax.broadcasted_iota(jnp.int32, (blk, blk), 1)).astype(F32)
        bias_s[0] = slope * rel
        bias_s[1] = -slope * jnp.abs(rel)
        bias_s[2] = -slope * rel

        first_map = lax.broadcasted_iota(jnp.int32, (blk, HEAD_LANES), 1) < half

        def norm_block(j, best):
            kb = keys(j).astype(F32)
            sq = kb * kb
            n0 = jnp.sqrt(jnp.sum(jnp.where(first_map, sq, 0.0), axis=1, keepdims=True))
            n1 = jnp.sqrt(jnp.sum(jnp.where(first_map, 0.0, sq), axis=1, keepdims=True))
            return jnp.maximum(best[0], jnp.max(n0)), jnp.maximum(best[1], jnp.max(n1))

        kmax = lax.fori_loop(0, n_kblk, norm_block, (jnp.float32(0.0), jnp.float32(0.0)))
        kmax_s[0] = kmax[0]
        kmax_s[1] = kmax[1]

    zeros = jnp.zeros((half, blk), BF16)
    q_s[0, 0:half, :] = qt_ref[0, 0, 0, 0:half, :]
    q_s[0, half:, :] = zeros
    q_s[1, 0:half, :] = zeros
    q_s[1, half:, :] = qt_ref[0, 0, 0, half:, :]
    m_s[...] = jnp.full(m_s.shape, -jnp.inf, F32)
    l_s[...] = jnp.zeros(l_s.shape, F32)
    acc_s[...] = jnp.zeros(acc_s.shape, F32)

    u_max = jnp.float32(0.0)
    for mp in range(2):
        u = _column_norms(q_s[mp]) * (kmax_s[mp] * NORM_MARGIN)
        u_s[mp] = u
        u_max = jnp.maximum(u_max, jnp.max(u))
    bounded = u_max <= BOUNDED_STABILISER_MAX

    def score(kblk, mp):
        return jnp.dot(kblk, q_s[mp], preferred_element_type=F32)

    def tile_bias(j):
        side = jnp.where(j < qi, 0, jnp.where(j == qi, 1, 2))
        return bias_s[side], -slope * (jnp.abs(j - qi) * blk).astype(F32)

    @pl.when(bounded)
    def _bounded_sweep():
        w = windows_ref[h]
        lo = jnp.maximum(qi - w, 0)
        hi = jnp.minimum(qi + w + 1, n_kblk)

        def step(s, mp, j, vt):
            bias, c = tile_bias(j)
            _bounded_softmax_step(s + bias + (c - u_s[mp]), vt, l_s, acc_s, mp)

        _sweep_key_blocks(lo, hi, keys, lambda j: vt_ref[0, 0, j], score, step, 2, s_s)

    @pl.when(jnp.logical_not(bounded))
    def _online_sweep():
        def step(s, mp, j, vt):
            bias, c = tile_bias(j)
            _online_softmax_step(s + bias, c, vt, m_s, l_s, acc_s, mp)

        _sweep_key_blocks(0, n_kblk, keys, lambda j: vt_ref[0, 0, j], score, step, 2, s_s)

    dl = lam_ref[...]
    lam = (jnp.exp(jnp.sum(dl[0:1] * dl[1:2], axis=-1, keepdims=True))
           - jnp.exp(jnp.sum(dl[2:3] * dl[3:4], axis=-1, keepdims=True)) + lambda_init)
    o = acc_s[0] * (1.0 / l_s[0]) - lam * (acc_s[1] * (1.0 / l_s[1]))
    ms = jnp.mean(o * o, axis=0, keepdims=True)
    y = o * lax.rsqrt(ms + EPS) * g_ref[...] * (1.0 - lambda_init)
    o_ref[0] = y.T.astype(o_ref.dtype)


def _alibi_windows(slopes, blk):
    dist = EXP2_UNDERFLOW / (slopes * LOG2E)
    return (jnp.floor((dist - 1.0) / blk) + 1.0).astype(jnp.int32)


def _diff_attention(qt, k, vt, slopes, diff_lambda, subln_g, lambda_init):
    b, nh, nb, _, blk = qt.shape
    s = nb * blk
    body = functools.partial(_diff_attn_kernel, lambda_init=lambda_init, n_kblk=nb)
    return pl.pallas_call(
        body,
        grid=(b, nh, nb),
        in_specs=[pl.BlockSpec(memory_space=pltpu.SMEM),
                  pl.BlockSpec(memory_space=pltpu.SMEM),
                  pl.BlockSpec((4, DIFF_HD), lambda bb, hh, qi: (0, 0)),
                  pl.BlockSpec((DIFF_V_HD, 1), lambda bb, hh, qi: (0, 0)),
                  pl.BlockSpec((1, 1, 1, HEAD_LANES, blk), lambda bb, hh, qi: (bb, hh, qi, 0, 0)),
                  pl.BlockSpec((1, 1, s, HEAD_LANES), lambda bb, hh, qi: (bb, hh, 0, 0)),
                  pl.BlockSpec((1, 1, nb, HEAD_LANES, blk), lambda bb, hh, qi: (bb, hh, 0, 0, 0))],
        out_specs=pl.BlockSpec((1, blk, HEAD_LANES), lambda bb, hh, qi: (bb, qi, hh)),
        out_shape=jax.ShapeDtypeStruct((b, s, nh * HEAD_LANES), BF16),
        scratch_shapes=[pltpu.VMEM((2, HEAD_LANES, blk), BF16),
                        pltpu.VMEM((3, blk, blk), F32),
                        pltpu.VMEM((DIFF_LOOKAHEAD, blk, blk), F32),
                        pltpu.VMEM((2, 1, blk), F32),
                        pltpu.VMEM((2, 1, blk), F32),
                        pltpu.VMEM((2, 1, blk), F32),
                        pltpu.VMEM((2, HEAD_LANES, blk), F32),
                        pltpu.SMEM((2,), F32)],
        compiler_params=pltpu.CompilerParams(dimension_semantics=("arbitrary",) * 3,
                                             vmem_limit_bytes=VMEM_LIMIT_BYTES),
        name="diff_attention",
    )(slopes, _alibi_windows(slopes, blk), diff_lambda, subln_g.reshape(DIFF_V_HD, 1), qt, k, vt)


def _gqa_attn_kernel(qt_ref, k_ref, vt_ref, o_ref, s_s, u_s, m_s, l_s, acc_s, kmax_s, *, n_kblk):
    blk = ATT_BLK
    qi = pl.program_id(2)

    def keys(j):
        return k_ref[0, 0, pl.ds(pl.multiple_of(j * blk, blk), blk), :]

    @pl.when(qi == 0)
    def _kv_head_init():
        def norm_block(j, best):
            kb = keys(j).astype(F32)
            return jnp.maximum(best, jnp.max(jnp.sqrt(jnp.sum(kb * kb, axis=1, keepdims=True))))

        kmax_s[0] = lax.fori_loop(0, n_kblk, norm_block, jnp.float32(0.0))

    m_s[...] = jnp.full(m_s.shape, -jnp.inf, F32)
    l_s[...] = jnp.zeros(l_s.shape, F32)
    acc_s[...] = jnp.zeros(acc_s.shape, F32)

    u_max = jnp.float32(0.0)
    for g in range(GQA_GROUP):
        u = _column_norms(qt_ref[0, g, 0]) * (kmax_s[0] * NORM_MARGIN)
        u_s[g] = u
        u_max = jnp.maximum(u_max, jnp.max(u))
    bounded = u_max <= BOUNDED_STABILISER_MAX

    def score(kblk, g):
        return jnp.dot(kblk, qt_ref[0, g, 0], preferred_element_type=F32)

    @pl.when(bounded)
    def _bounded_sweep():
        def step(s, g, j, vt):
            _bounded_softmax_step(s - u_s[g], vt, l_s, acc_s, g)

        _sweep_key_blocks(0, n_kblk, keys, lambda j: vt_ref[0, 0, j], score, step, GQA_GROUP, s_s, unroll=2)

    @pl.when(jnp.logical_not(bounded))
    def _online_sweep():
        def step(s, g, j, vt):
            _online_softmax_step(s, 0.0, vt, m_s, l_s, acc_s, g)

        _sweep_key_blocks(0, n_kblk, keys, lambda j: vt_ref[0, 0, j], score, step, GQA_GROUP, s_s)

    for g in range(GQA_GROUP):
        o = acc_s[g] * (1.0 / l_s[g])
        o_ref[0, :, g * HEAD_LANES:(g + 1) * HEAD_LANES] = o.T.astype(o_ref.dtype)


def _gqa_attention(qt, k, vt):
    b, nh, nb, _, blk = qt.shape
    s = nb * blk
    nkv = nh // GQA_GROUP
    body = functools.partial(_gqa_attn_kernel, n_kblk=nb)
    return pl.pallas_call(
        body,
        grid=(b, nkv, nb),
        in_specs=[pl.BlockSpec((1, GQA_GROUP, 1, HEAD_LANES, blk), lambda bb, kv, qi: (bb, kv, qi, 0, 0)),
                  pl.BlockSpec((1, 1, s, HEAD_LANES), lambda bb, kv, qi: (bb, kv, 0, 0)),
                  pl.BlockSpec((1, 1, nb, HEAD_LANES, blk), lambda bb, kv, qi: (bb, kv, 0, 0, 0))],
        out_specs=pl.BlockSpec((1, blk, GQA_GROUP * HEAD_LANES), lambda bb, kv, qi: (bb, qi, kv)),
        out_shape=jax.ShapeDtypeStruct((b, s, nh * HEAD_LANES), BF16),
        scratch_shapes=[pltpu.VMEM((GQA_LOOKAHEAD, blk, blk), F32),
                        pltpu.VMEM((GQA_GROUP, 1, blk), F32),
                        pltpu.VMEM((GQA_GROUP, 1, blk), F32),
                        pltpu.VMEM((GQA_GROUP, 1, blk), F32),
                        pltpu.VMEM((GQA_GROUP, HEAD_LANES, blk), F32),
                        pltpu.SMEM((1,), F32)],
        compiler_params=pltpu.CompilerParams(dimension_semantics=("arbitrary",) * 3,
                                             vmem_limit_bytes=VMEM_LIMIT_BYTES),
        name="gqa_attention",
    )(qt, k, vt)


def _merge_kernel(oa_ref, ob_ref, wa_ref, wb_ref, ga_ref, gb_ref, o_ref):
    a = jnp.dot(oa_ref[...], wa_ref[...], preferred_element_type=F32)
    bb = jnp.dot(ob_ref[...], wb_ref[...], preferred_element_type=F32)
    o_ref[...] = (ga_ref[...] * a + gb_ref[...] * bb).astype(o_ref.dtype)


def _merge(oa, ob, wa, wb, gate, tm=1024, tn=512):
    t = oa.shape[0]
    gb0 = D_MODEL // tn
    return pl.pallas_call(
        _merge_kernel,
        grid=(t // tm, D_MODEL // tn),
        in_specs=[pl.BlockSpec((tm, DIFF_WIDTH), lambda i, j: (i, 0)),
                  pl.BlockSpec((tm, GQA_WIDTH), lambda i, j: (i, 0)),
                  pl.BlockSpec((DIFF_WIDTH, tn), lambda i, j: (0, j)),
                  pl.BlockSpec((GQA_WIDTH, tn), lambda i, j: (0, j)),
                  pl.BlockSpec((tm, tn), lambda i, j: (i, j)),
                  pl.BlockSpec((tm, tn), lambda i, j: (i, gb0 + j))],
        out_specs=pl.BlockSpec((tm, tn), lambda i, j: (i, j)),
        out_shape=jax.ShapeDtypeStruct((t, D_MODEL), BF16),
        compiler_params=_params(2),
        name="merge",
    )(oa, ob, wa, wb, gate, gate)


def _out_proj_kernel(a_ref, w_ref, x_ref, g_ref, gn_ref, o_ref, h_ref):
    m = jnp.dot(a_ref[...], w_ref[...], preferred_element_type=F32)
    x = x_ref[...] + _rms(m, g_ref[...])
    o_ref[...] = x
    h_ref[...] = _rms(x, gn_ref[...]).astype(h_ref.dtype)


def _out_proj(a, w, x2d, g, g_next, tm=512):
    t = a.shape[0]
    return pl.pallas_call(
        _out_proj_kernel,
        grid=(t // tm,),
        in_specs=[pl.BlockSpec((tm, D_MODEL), lambda i: (i, 0)),
                  pl.BlockSpec((D_MODEL, D_MODEL), lambda i: (0, 0)),
                  pl.BlockSpec((tm, D_MODEL), lambda i: (i, 0)),
                  pl.BlockSpec((1, D_MODEL), lambda i: (0, 0)),
                  pl.BlockSpec((1, D_MODEL), lambda i: (0, 0))],
        out_specs=[pl.BlockSpec((tm, D_MODEL), lambda i: (i, 0)),
                   pl.BlockSpec((tm, D_MODEL), lambda i: (i, 0))],
        out_shape=[jax.ShapeDtypeStruct((t, D_MODEL), F32),
                   jax.ShapeDtypeStruct((t, D_MODEL), BF16)],
        compiler_params=_params(1),
        name="out_proj",
    )(a, w, x2d, g.reshape(1, D_MODEL), g_next.reshape(1, D_MODEL))


def _swiglu_kernel(h_ref, wg_ref, wu_ref, o_ref):
    h = h_ref[...]
    gte = jnp.dot(h, wg_ref[...], preferred_element_type=F32)
    up = jnp.dot(h, wu_ref[...], preferred_element_type=F32)
    o_ref[...] = (jax.nn.silu(gte) * up).astype(o_ref.dtype)


def _swiglu(h2d, w_gate_up, tm=1024, tn=512):
    t = h2d.shape[0]
    ub0 = D_FF // tn
    return pl.pallas_call(
        _swiglu_kernel,
        grid=(t // tm, D_FF // tn),
        in_specs=[pl.BlockSpec((tm, D_MODEL), lambda i, j: (i, 0)),
                  pl.BlockSpec((D_MODEL, tn), lambda i, j: (0, j)),
                  pl.BlockSpec((D_MODEL, tn), lambda i, j: (0, ub0 + j))],
        out_specs=pl.BlockSpec((tm, tn), lambda i, j: (i, j)),
        out_shape=jax.ShapeDtypeStruct((t, D_FF), BF16),
        compiler_params=_params(2),
        name="swiglu",
    )(h2d, w_gate_up, w_gate_up)


def _down_kernel(a_ref, w_ref, x_ref, g_ref, *rest, with_next):
    if with_next:
        gn_ref, o_ref, h_ref, acc_ref = rest
    else:
        o_ref, acc_ref = rest
    kk = pl.program_id(1)

    @pl.when(kk == 0)
    def _init():
        acc_ref[...] = jnp.zeros(acc_ref.shape, F32)

    acc_ref[...] += jnp.dot(a_ref[...], w_ref[...], preferred_element_type=F32)

    @pl.when(kk == pl.num_programs(1) - 1)
    def _finalize():
        x = x_ref[...] + _rms(acc_ref[...], g_ref[...])
        o_ref[...] = x
        if with_next:
            h_ref[...] = _rms(x, gn_ref[...]).astype(h_ref.dtype)


def _down_proj(a, w, x2d, g, g_next=None, tm=512, tk=D_FF // 4):
    t = a.shape[0]
    with_next = g_next is not None
    row_spec = pl.BlockSpec((tm, D_MODEL), lambda i, kk: (i, 0))
    gain_spec = pl.BlockSpec((1, D_MODEL), lambda i, kk: (0, 0))
    in_specs = [pl.BlockSpec((tm, tk), lambda i, kk: (i, kk)),
                pl.BlockSpec((tk, D_MODEL), lambda i, kk: (kk, 0)),
                row_spec, gain_spec]
    args = [a, w, x2d, g.reshape(1, D_MODEL)]
    out_specs = [row_spec]
    out_shape = [jax.ShapeDtypeStruct((t, D_MODEL), F32)]
    if with_next:
        in_specs.append(gain_spec)
        args.append(g_next.reshape(1, D_MODEL))
        out_specs.append(row_spec)
        out_shape.append(jax.ShapeDtypeStruct((t, D_MODEL), BF16))
    out = pl.pallas_call(
        functools.partial(_down_kernel, with_next=with_next),
        grid=(t // tm, D_FF // tk),
        in_specs=in_specs, out_specs=out_specs, out_shape=out_shape,
        scratch_shapes=[pltpu.VMEM((tm, D_MODEL), F32)],
        compiler_params=_params(2),
        name="down_proj",
    )(*args)
    return (out[0], out[1]) if with_next else (out[0], None)


def _rope_tables(s):
    rows = s // GRID_W
    row = jnp.repeat(jnp.arange(rows, dtype=F32), GRID_W)
    col = jnp.tile(jnp.arange(GRID_W, dtype=F32), rows)
    quarter = GQA_HD // 4
    freqs = ROPE_THETA ** (-jnp.arange(quarter, dtype=F32) / quarter)
    ang_r = row[:, None] * freqs[None, :]
    ang_c = col[:, None] * freqs[None, :]
    cos = jnp.concatenate([jnp.cos(ang_r), jnp.cos(ang_r), jnp.cos(ang_c), jnp.cos(ang_c)], axis=-1)
    sin = jnp.concatenate([-jnp.sin(ang_r), jnp.sin(ang_r), -jnp.sin(ang_c), jnp.sin(ang_c)], axis=-1)
    return cos, sin


def _trunk(x, weights):
    b, s, _ = x.shape
    x2d = x.reshape(b * s, D_MODEL)
    cos, sin = _rope_tables(s)
    slopes = jnp.exp2(-8.0 * jnp.arange(1, DIFF_HEADS + 1, dtype=F32) / DIFF_HEADS)
    h = _prenorm(x2d, weights[0][0])
    for l in range(DEPTH):
        (mix_pre_g, w_in, diff_lambda, diff_subln_g, gqa_qnorm_g, gqa_knorm_g, w_proj_a, w_proj_b,
         w_out, mix_post_g, ffn_pre_g, w_gate_up, w_down, ffn_post_g) = [w[l] for w in weights]
        lambda_init = 0.8 - 0.6 * math.exp(-0.3 * l)
        dqt = _proj_heads(h, w_in, b, s, OFF_DQ, DIFF_QK_WIDTH, transpose=True,
                          scale=DIFF_HD ** -0.5 * LOG2E)
        dk = _proj_heads(h, w_in, b, s, OFF_DK, DIFF_QK_WIDTH, transpose=False)
        dvt = _proj_heads(h, w_in, b, s, OFF_DV, DIFF_WIDTH, transpose=True)
        o_a = _diff_attention(dqt, dk, dvt, slopes, diff_lambda, diff_subln_g, lambda_init)
        gqt = _proj_heads(h, w_in, b, s, OFF_GQ, GQA_WIDTH, transpose=True, scale=GQA_HD ** -0.5 * LOG2E,
                          rope=(gqa_qnorm_g, cos, sin))
        gk = _proj_heads(h, w_in, b, s, OFF_GK, GQA_KV_WIDTH, transpose=False,
                         rope=(gqa_knorm_g, cos, sin))
        gvt = _proj_heads(h, w_in, b, s, OFF_GV, GQA_KV_WIDTH, transpose=True)
        o_b = _gqa_attention(gqt, gk, gvt)
        gate = _proj_gate(h, w_in)
        merged = _merge(o_a.reshape(b * s, DIFF_WIDTH), o_b.reshape(b * s, GQA_WIDTH),
                        w_proj_a, w_proj_b, gate)
        x2d, h = _out_proj(merged, w_out, x2d, mix_post_g, ffn_pre_g)
        act = _swiglu(h, w_gate_up)
        next_pre_g = weights[0][l + 1] if l + 1 < DEPTH else None
        x2d, h = _down_proj(act, w_down, x2d, ffn_post_g, next_pre_g)
    return x2d.reshape(b, s, D_MODEL)


def kernel(x_prompt, x_sample, mix_pre_g, w_in, diff_lambda, diff_subln_g, gqa_qnorm_g, gqa_knorm_g,
           w_proj_a, w_proj_b, w_out, mix_post_g, ffn_pre_g, w_gate_up, w_down, ffn_post_g):
    weights = (mix_pre_g, w_in.astype(BF16), diff_lambda, diff_subln_g, gqa_qnorm_g, gqa_knorm_g,
               w_proj_a.astype(BF16), w_proj_b.astype(BF16), w_out.astype(BF16), mix_post_g,
               ffn_pre_g, w_gate_up.astype(BF16), w_down.astype(BF16), ffn_post_g)
    return (_trunk(x_prompt, weights), _trunk(x_sample, weights))
```

```python
import functools
import math

import jax
import jax.numpy as jnp
from jax import lax
from jax.experimental import pallas as pl
from jax.experimental.pallas import tpu as pltpu

D_MODEL = 2048
DEPTH = 2
GRID_W = 64
EPS = 1e-6
DIFF_HEADS = 8
DIFF_HD = 64
DIFF_V_HD = 128
DIFF_QK_WIDTH = 1024
DIFF_WIDTH = 1024
GQA_HEADS = 8
GQA_KV_HEADS = 2
GQA_GROUP = GQA_HEADS // GQA_KV_HEADS
GQA_HD = 128
GQA_WIDTH = 1024
GQA_KV_WIDTH = 256
ROPE_THETA = 10000.0
GATE_WIDTH = 2 * D_MODEL
D_FF = 5632

HEAD_LANES = 128
OFF_DQ = 0
OFF_DK = OFF_DQ + DIFF_QK_WIDTH
OFF_DV = OFF_DK + DIFF_QK_WIDTH
OFF_GQ = OFF_DV + DIFF_WIDTH
OFF_GK = OFF_GQ + GQA_WIDTH
OFF_GV = OFF_GK + GQA_KV_WIDTH
OFF_GATE = OFF_GV + GQA_KV_WIDTH

VMEM_LIMIT_BYTES = 52 * 1024 * 1024
ATT_BLK = 512
DIFF_LOOKAHEAD = 2
DIFF_QBLOCKS = 1
GQA_LOOKAHEAD = 1
LOG2E = math.log2(math.e)

BF16 = jnp.bfloat16
F32 = jnp.float32


def _params(n_grid):
    return pltpu.CompilerParams(
        dimension_semantics=("parallel",) * (n_grid - 1) + ("arbitrary",),
        vmem_limit_bytes=VMEM_LIMIT_BYTES)


def _rms(x, g):
    ms = jnp.mean(x * x, axis=-1, keepdims=True)
    return x * lax.rsqrt(ms + EPS) * g


def _prenorm_kernel(x_ref, g_ref, o_ref):
    o_ref[...] = _rms(x_ref[...], g_ref[...]).astype(o_ref.dtype)


def _prenorm(x2d, g, tm=1024):
    t = x2d.shape[0]
    return pl.pallas_call(
        _prenorm_kernel,
        grid=(t // tm,),
        in_specs=[pl.BlockSpec((tm, D_MODEL), lambda i: (i, 0)),
                  pl.BlockSpec((1, D_MODEL), lambda i: (0, 0))],
        out_specs=pl.BlockSpec((tm, D_MODEL), lambda i: (i, 0)),
        out_shape=jax.ShapeDtypeStruct((t, D_MODEL), BF16),
        compiler_params=_params(1),
        name="prenorm",
    )(x2d, g.reshape(1, D_MODEL))


def _rope_lane_order():
    quarter = GQA_HD // 4
    return [i + off for off in (0, 2 * quarter, quarter, 3 * quarter) for i in range(quarter)]


def _rope(y, cos, sin_signed):
    return y * cos + pltpu.roll(y, GQA_HD // 2, 1) * sin_signed


def _head_mean_square(y):
    sq = (y * y).astype(BF16)
    ones = jnp.ones((HEAD_LANES, HEAD_LANES), BF16)
    return jnp.dot(sq, ones, preferred_element_type=F32) * (1.0 / HEAD_LANES)


def _store_head(o_ref, hh, y, transpose):
    if transpose:
        for c in range(y.shape[0] // ATT_BLK):
            o_ref[0, hh, c] = y[c * ATT_BLK:(c + 1) * ATT_BLK].T.astype(o_ref.dtype)
    else:
        o_ref[0, hh] = y.astype(o_ref.dtype)


def _seg_heads_kernel(h_ref, w_ref, o_ref, *, scale, transpose, n_heads):
    acc = jnp.dot(h_ref[...], w_ref[...], preferred_element_type=F32)
    for hh in range(n_heads):
        y = acc[:, hh * HEAD_LANES:(hh + 1) * HEAD_LANES]
        if scale != 1.0:
            y = y * scale
        _store_head(o_ref, hh, y, transpose)


def _seg_rope_kernel(h_ref, w_ref, g_ref, cos_ref, sin_ref, o_ref, *, scale, transpose, n_heads):
    acc = jnp.dot(h_ref[...], w_ref[...], preferred_element_type=F32)
    cos = cos_ref[...]
    sin = sin_ref[...]
    g = g_ref[...]
    for hh in range(n_heads):
        y = acc[:, hh * HEAD_LANES:(hh + 1) * HEAD_LANES]
        y = y * lax.rsqrt(_head_mean_square(y) + EPS) * g
        y = _rope(y, cos, sin)
        if scale != 1.0:
            y = y * scale
        _store_head(o_ref, hh, y, transpose)


def _seg_gate_kernel(h_ref, w_ref, o_ref):
    acc = jnp.dot(h_ref[...], w_ref[...], preferred_element_type=F32)
    o_ref[...] = jax.nn.sigmoid(acc).astype(o_ref.dtype)


def _proj_heads(h2d, w_in, b, s, col0, ncols, *, transpose, scale=1.0, rope=None, tm=1024, tn=1024):
    tn = min(tn, ncols)
    n_heads_blk = tn // HEAD_LANES
    n_heads = ncols // HEAD_LANES
    nsb = s // tm
    grid = (b * nsb, ncols // tn)
    cb0 = col0 // tn
    assert col0 % tn == 0 and s % tm == 0
    in_specs = [pl.BlockSpec((tm, D_MODEL), lambda i, j: (i, 0)),
                pl.BlockSpec((D_MODEL, tn), lambda i, j: (0, cb0 + j))]
    args = [h2d, w_in]
    if rope is None:
        body = functools.partial(_seg_heads_kernel, scale=scale, transpose=transpose, n_heads=n_heads_blk)
    else:
        gain, cos, sin = rope
        in_specs += [pl.BlockSpec((1, HEAD_LANES), lambda i, j: (0, 0)),
                     pl.BlockSpec((tm, HEAD_LANES), lambda i, j: (i % nsb, 0)),
                     pl.BlockSpec((tm, HEAD_LANES), lambda i, j: (i % nsb, 0))]
        args += [gain.reshape(1, HEAD_LANES), cos, sin]
        body = functools.partial(_seg_rope_kernel, scale=scale, transpose=transpose, n_heads=n_heads_blk)
    if transpose:
        out_shape = (b, n_heads, s // ATT_BLK, HEAD_LANES, ATT_BLK)
        out_spec = pl.BlockSpec((1, n_heads_blk, tm // ATT_BLK, HEAD_LANES, ATT_BLK),
                                lambda i, j: (i // nsb, j, i % nsb, 0, 0))
    else:
        out_shape = (b, n_heads, s, HEAD_LANES)
        out_spec = pl.BlockSpec((1, n_heads_blk, tm, HEAD_LANES), lambda i, j: (i // nsb, j, i % nsb, 0))
    return pl.pallas_call(
        body, grid=grid, in_specs=in_specs, out_specs=out_spec,
        out_shape=jax.ShapeDtypeStruct(out_shape, BF16),
        compiler_params=_params(2),
        name="proj_heads",
    )(*args)


def _proj_gate(h2d, w_gate, tm=1024, tn=1024):
    t = h2d.shape[0]
    return pl.pallas_call(
        _seg_gate_kernel,
        grid=(t // tm, GATE_WIDTH // tn),
        in_specs=[pl.BlockSpec((tm, D_MODEL), lambda i, j: (i, 0)),
                  pl.BlockSpec((D_MODEL, tn), lambda i, j: (0, j))],
        out_specs=pl.BlockSpec((tm, tn), lambda i, j: (i, j)),
        out_shape=jax.ShapeDtypeStruct((t, GATE_WIDTH), F32),
        compiler_params=_params(2),
        name="proj_gate",
    )(h2d, w_gate)


BOUNDED_STABILISER_MAX = 32.0
NORM_MARGIN = 1.02
EXP2_UNDERFLOW = 127.0


def _online_softmax_step(s, c, vt, m_ref, l_ref, acc_ref, idx):
    m_old = m_ref[idx]
    m_new = jnp.maximum(m_old, jnp.max(s, axis=0, keepdims=True) + c)
    p = jnp.exp2(s - (m_new - c))
    alpha = jnp.exp2(m_old - m_new)
    l_ref[idx] = alpha * l_ref[idx] + jnp.sum(p, axis=0, keepdims=True)
    acc_ref[idx] = alpha * acc_ref[idx] + jnp.dot(vt, p.astype(BF16), preferred_element_type=F32)
    m_ref[idx] = m_new


def _bounded_softmax_step(t, vt, l_ref, acc_ref, idx):
    p = jnp.exp2(t)
    l_ref[idx] = l_ref[idx] + jnp.sum(p, axis=0, keepdims=True)
    acc_ref[idx] = acc_ref[idx] + jnp.dot(vt, p.astype(BF16), preferred_element_type=F32)


def _sweep_key_blocks(lo, hi, keys, values_t, score, step, n_chains, s_s, unroll=1):
    ahead = s_s.shape[0]
    k_first = keys(lo)
    for i in range(ahead):
        s_s[i] = score(k_first, i)

    def key_blocks(j0, count):
        queue = [s_s[i] for i in range(ahead)]
        for t in range(count):
            j = j0 + t
            k = keys(j)
            k_next = keys(jnp.minimum(j + 1, hi - 1))
            vt = values_t(j)
            for ci in range(n_chains):
                nxt = ci + ahead
                queue.append(score(k, nxt) if nxt < n_chains else score(k_next, nxt - n_chains))
                step(queue.pop(0), ci, j, vt)
        for i in range(ahead):
            s_s[i] = queue[i]

    n_groups = (hi - lo) // unroll

    def group(i, carry):
        key_blocks(lo + i * unroll, unroll)
        return carry

    lax.fori_loop(0, n_groups, group, 0)
    if unroll > 1 and not (isinstance(n_groups, int) and n_groups * unroll == hi - lo):
        def single(j, carry):
            key_blocks(j, 1)
            return carry

        lax.fori_loop(lo + n_groups * unroll, hi, single, 0)


def _column_norms(x_t):
    x = x_t.astype(F32)
    return jnp.sqrt(jnp.sum(x * x, axis=0, keepdims=True))


def _diff_attn_kernel(slopes_ref, windows_ref, lam_ref, g_ref, qt_ref, k_ref, vt_ref, o_ref,
                      q_s, bias_s, s_s, u_s, m_s, l_s, acc_s, kmax_s, *, lambda_init, n_kblk):
    blk = ATT_BLK
    n_chains = 2 * DIFF_QBLOCKS
    h = pl.program_id(1)
    qi = pl.program_id(2)
    q0 = qi * DIFF_QBLOCKS
    slope = slopes_ref[h] * LOG2E
    half = DIFF_HD

    def keys(j):
        return k_ref[0, 0, pl.ds(pl.multiple_of(j * blk, blk), blk), :]

    @pl.when(qi == 0)
    def _head_init():
        rel = (lax.broadcasted_iota(jnp.int32, (blk, blk), 0)
               - lax.broadcasted_iota(jnp.int32, (blk, blk), 1)).astype(F32)
        bias_s[0] = slope * rel
        bias_s[1] = -slope * jnp.abs(rel)
        bias_s[2] = -slope * rel

        first_map = lax.broadcasted_iota(jnp.int32, (blk, HEAD_LANES), 1) < half

        def norm_block(j, best):
            kb = keys(j).astype(F32)
            sq = kb * kb
            n0 = jnp.sqrt(jnp.sum(jnp.where(first_map, sq, 0.0), axis=1, keepdims=True))
            n1 = jnp.sqrt(jnp.sum(jnp.where(first_map, 0.0, sq), axis=1, keepdims=True))
            return jnp.maximum(best[0], jnp.max(n0)), jnp.maximum(best[1], jnp.max(n1))

        kmax = lax.fori_loop(0, n_kblk, norm_block, (jnp.float32(0.0), jnp.float32(0.0)))
        kmax_s[0] = kmax[0]
        kmax_s[1] = kmax[1]

    zeros = jnp.zeros((half, blk), BF16)
    for qb in range(DIFF_QBLOCKS):
        q_s[2 * qb, 0:half, :] = qt_ref[0, 0, qb, 0:half, :]
        q_s[2 * qb, half:, :] = zeros
        q_s[2 * qb + 1, 0:half, :] = zeros
        q_s[2 * qb + 1, half:, :] = qt_ref[0, 0, qb, half:, :]
    m_s[...] = jnp.full(m_s.shape, -jnp.inf, F32)
    l_s[...] = jnp.zeros(l_s.shape, F32)
    acc_s[...] = jnp.zeros(acc_s.shape, F32)

    u_max = jnp.float32(0.0)
    for ci in range(n_chains):
        u = _column_norms(q_s[ci]) * (kmax_s[ci % 2] * NORM_MARGIN)
        u_s[ci] = u
        u_max = jnp.maximum(u_max, jnp.max(u))
    bounded = u_max <= BOUNDED_STABILISER_MAX

    def score(kblk, ci):
        return jnp.dot(kblk, q_s[ci], preferred_element_type=F32)

    neg_slope_row = jnp.full((1, blk), slopes_ref[h], F32) * (-LOG2E)

    def tile_bias(j, ci):
        qblk = q0 + ci // 2
        side = jnp.where(j < qblk, 0, jnp.where(j == qblk, 1, 2))
        origin_gap = jnp.full((1, blk), jnp.abs(j - qblk) * blk, jnp.int32).astype(F32)
        return bias_s[side], neg_slope_row * origin_gap

    @pl.when(bounded)
    def _bounded_sweep():
        w = windows_ref[h]
        lo = jnp.maximum(q0 - w, 0)
        hi = jnp.minimum(q0 + DIFF_QBLOCKS + w, n_kblk)

        def step(s, ci, j, vt):
            bias, c = tile_bias(j, ci)
            _bounded_softmax_step(s + bias + (c - u_s[ci]), vt, l_s, acc_s, ci)

        _sweep_key_blocks(lo, hi, keys, lambda j: vt_ref[0, 0, j], score, step, n_chains, s_s)

    @pl.when(jnp.logical_not(bounded))
    def _online_sweep():
        def step(s, ci, j, vt):
            bias, c = tile_bias(j, ci)
            _online_softmax_step(s + bias, c, vt, m_s, l_s, acc_s, ci)

        _sweep_key_blocks(0, n_kblk, keys, lambda j: vt_ref[0, 0, j], score, step, n_chains, s_s)

    dl = lam_ref[...]
    lam = (jnp.exp(jnp.sum(dl[0:1] * dl[1:2], axis=-1, keepdims=True))
           - jnp.exp(jnp.sum(dl[2:3] * dl[3:4], axis=-1, keepdims=True)) + lambda_init)
    for qb in range(DIFF_QBLOCKS):
        c0, c1 = 2 * qb, 2 * qb + 1
        o = acc_s[c0] * (1.0 / l_s[c0]) - lam * (acc_s[c1] * (1.0 / l_s[c1]))
        ms = jnp.mean(o * o, axis=0, keepdims=True)
        y = o * lax.rsqrt(ms + EPS) * g_ref[...] * (1.0 - lambda_init)
        o_ref[0, qb * blk:(qb + 1) * blk, :] = y.T.astype(o_ref.dtype)


def _alibi_windows(slopes, blk):
    dist = EXP2_UNDERFLOW / (slopes * LOG2E)
    return (jnp.floor((dist - 1.0) / blk) + 1.0).astype(jnp.int32)


def _diff_attention(qt, k, vt, slopes, diff_lambda, subln_g, lambda_init):
    b, nh, nb, _, blk = qt.shape
    s = nb * blk
    body = functools.partial(_diff_attn_kernel, lambda_init=lambda_init, n_kblk=nb)
    qbs = DIFF_QBLOCKS
    n_chains = 2 * qbs
    return pl.pallas_call(
        body,
        grid=(b, nh, nb // qbs),
        in_specs=[pl.BlockSpec(memory_space=pltpu.SMEM),
                  pl.BlockSpec(memory_space=pltpu.SMEM),
                  pl.BlockSpec((4, DIFF_HD), lambda bb, hh, qi: (0, 0)),
                  pl.BlockSpec((DIFF_V_HD, 1), lambda bb, hh, qi: (0, 0)),
                  pl.BlockSpec((1, 1, qbs, HEAD_LANES, blk), lambda bb, hh, qi: (bb, hh, qi, 0, 0)),
                  pl.BlockSpec((1, 1, s, HEAD_LANES), lambda bb, hh, qi: (bb, hh, 0, 0)),
                  pl.BlockSpec((1, 1, nb, HEAD_LANES, blk), lambda bb, hh, qi: (bb, hh, 0, 0, 0))],
        out_specs=pl.BlockSpec((1, qbs * blk, HEAD_LANES), lambda bb, hh, qi: (bb, qi, hh)),
        out_shape=jax.ShapeDtypeStruct((b, s, nh * HEAD_LANES), BF16),
        scratch_shapes=[pltpu.VMEM((n_chains, HEAD_LANES, blk), BF16),
                        pltpu.VMEM((3, blk, blk), F32),
                        pltpu.VMEM((DIFF_LOOKAHEAD, blk, blk), F32),
                        pltpu.VMEM((n_chains, 1, blk), F32),
                        pltpu.VMEM((n_chains, 1, blk), F32),
                        pltpu.VMEM((n_chains, 1, blk), F32),
                        pltpu.VMEM((n_chains, HEAD_LANES, blk), F32),
                        pltpu.SMEM((2,), F32)],
        compiler_params=pltpu.CompilerParams(dimension_semantics=("arbitrary",) * 3,
                                             vmem_limit_bytes=VMEM_LIMIT_BYTES),
        name="diff_attention",
    )(slopes, _alibi_windows(slopes, blk), diff_lambda, subln_g.reshape(DIFF_V_HD, 1), qt, k, vt)


def _gqa_attn_kernel(qt_ref, k_ref, vt_ref, o_ref, s_s, u_s, m_s, l_s, acc_s, kmax_s, *, n_kblk):
    blk = ATT_BLK
    qi = pl.program_id(2)

    def keys(j):
        return k_ref[0, 0, pl.ds(pl.multiple_of(j * blk, blk), blk), :]

    @pl.when(qi == 0)
    def _kv_head_init():
        def norm_block(j, best):
            kb = keys(j).astype(F32)
            return jnp.maximum(best, jnp.max(jnp.sqrt(jnp.sum(kb * kb, axis=1, keepdims=True))))

        kmax_s[0] = lax.fori_loop(0, n_kblk, norm_block, jnp.float32(0.0))

    m_s[...] = jnp.full(m_s.shape, -jnp.inf, F32)
    l_s[...] = jnp.zeros(l_s.shape, F32)
    acc_s[...] = jnp.zeros(acc_s.shape, F32)

    u_max = jnp.float32(0.0)
    for g in range(GQA_GROUP):
        u = _column_norms(qt_ref[0, g, 0]) * (kmax_s[0] * NORM_MARGIN)
        u_s[g] = u
        u_max = jnp.maximum(u_max, jnp.max(u))
    bounded = u_max <= BOUNDED_STABILISER_MAX

    def score(kblk, g):
        return jnp.dot(kblk, qt_ref[0, g, 0], preferred_element_type=F32)

    @pl.when(bounded)
    def _bounded_sweep():
        def step(s, g, j, vt):
            _bounded_softmax_step(s - u_s[g], vt, l_s, acc_s, g)

        _sweep_key_blocks(0, n_kblk, keys, lambda j: vt_ref[0, 0, j], score, step, GQA_GROUP, s_s, unroll=2)

    @pl.when(jnp.logical_not(bounded))
    def _online_sweep():
        def step(s, g, j, vt):
            _online_softmax_step(s, 0.0, vt, m_s, l_s, acc_s, g)

        _sweep_key_blocks(0, n_kblk, keys, lambda j: vt_ref[0, 0, j], score, step, GQA_GROUP, s_s)

    for g in range(GQA_GROUP):
        o = acc_s[g] * (1.0 / l_s[g])
        o_ref[0, :, g * HEAD_LANES:(g + 1) * HEAD_LANES] = o.T.astype(o_ref.dtype)


def _gqa_attention(qt, k, vt):
    b, nh, nb, _, blk = qt.shape
    s = nb * blk
    nkv = nh // GQA_GROUP
    body = functools.partial(_gqa_attn_kernel, n_kblk=nb)
    return pl.pallas_call(
        body,
        grid=(b, nkv, nb),
        in_specs=[pl.BlockSpec((1, GQA_GROUP, 1, HEAD_LANES, blk), lambda bb, kv, qi: (bb, kv, qi, 0, 0)),
                  pl.BlockSpec((1, 1, s, HEAD_LANES), lambda bb, kv, qi: (bb, kv, 0, 0)),
                  pl.BlockSpec((1, 1, nb, HEAD_LANES, blk), lambda bb, kv, qi: (bb, kv, 0, 0, 0))],
        out_specs=pl.BlockSpec((1, blk, GQA_GROUP * HEAD_LANES), lambda bb, kv, qi: (bb, qi, kv)),
        out_shape=jax.ShapeDtypeStruct((b, s, nh * HEAD_LANES), BF16),
        scratch_shapes=[pltpu.VMEM((GQA_LOOKAHEAD, blk, blk), F32),
                        pltpu.VMEM((GQA_GROUP, 1, blk), F32),
                        pltpu.VMEM((GQA_GROUP, 1, blk), F32),
                        pltpu.VMEM((GQA_GROUP, 1, blk), F32),
                        pltpu.VMEM((GQA_GROUP, HEAD_LANES, blk), F32),
                        pltpu.SMEM((1,), F32)],
        compiler_params=pltpu.CompilerParams(dimension_semantics=("arbitrary",) * 3,
                                             vmem_limit_bytes=VMEM_LIMIT_BYTES),
        name="gqa_attention",
    )(qt, k, vt)


def _merge_kernel(oa_ref, ob_ref, wa_ref, wb_ref, ga_ref, gb_ref, o_ref):
    a = jnp.dot(oa_ref[...], wa_ref[...], preferred_element_type=F32)
    bb = jnp.dot(ob_ref[...], wb_ref[...], preferred_element_type=F32)
    o_ref[...] = (ga_ref[...] * a + gb_ref[...] * bb).astype(o_ref.dtype)


def _merge(oa, ob, wa, wb, gate, tm=1024, tn=1024):
    t = oa.shape[0]
    gb0 = D_MODEL // tn
    return pl.pallas_call(
        _merge_kernel,
        grid=(t // tm, D_MODEL // tn),
        in_specs=[pl.BlockSpec((tm, DIFF_WIDTH), lambda i, j: (i, 0)),
                  pl.BlockSpec((tm, GQA_WIDTH), lambda i, j: (i, 0)),
                  pl.BlockSpec((DIFF_WIDTH, tn), lambda i, j: (0, j)),
                  pl.BlockSpec((GQA_WIDTH, tn), lambda i, j: (0, j)),
                  pl.BlockSpec((tm, tn), lambda i, j: (i, j)),
                  pl.BlockSpec((tm, tn), lambda i, j: (i, gb0 + j))],
        out_specs=pl.BlockSpec((tm, tn), lambda i, j: (i, j)),
        out_shape=jax.ShapeDtypeStruct((t, D_MODEL), BF16),
        compiler_params=_params(2),
        name="merge",
    )(oa, ob, wa, wb, gate, gate)


def _out_proj_kernel(a_ref, w_ref, x_ref, g_ref, gn_ref, o_ref, h_ref):
    m = jnp.dot(a_ref[...], w_ref[...], preferred_element_type=F32)
    x = x_ref[...] + _rms(m, g_ref[...])
    o_ref[...] = x
    h_ref[...] = _rms(x, gn_ref[...]).astype(h_ref.dtype)


def _out_proj(a, w, x2d, g, g_next, tm=512):
    t = a.shape[0]
    return pl.pallas_call(
        _out_proj_kernel,
        grid=(t // tm,),
        in_specs=[pl.BlockSpec((tm, D_MODEL), lambda i: (i, 0)),
                  pl.BlockSpec((D_MODEL, D_MODEL), lambda i: (0, 0)),
                  pl.BlockSpec((tm, D_MODEL), lambda i: (i, 0)),
                  pl.BlockSpec((1, D_MODEL), lambda i: (0, 0)),
                  pl.BlockSpec((1, D_MODEL), lambda i: (0, 0))],
        out_specs=[pl.BlockSpec((tm, D_MODEL), lambda i: (i, 0)),
                   pl.BlockSpec((tm, D_MODEL), lambda i: (i, 0))],
        out_shape=[jax.ShapeDtypeStruct((t, D_MODEL), F32),
                   jax.ShapeDtypeStruct((t, D_MODEL), BF16)],
        compiler_params=_params(1),
        name="out_proj",
    )(a, w, x2d, g.reshape(1, D_MODEL), g_next.reshape(1, D_MODEL))


def _swiglu_kernel(h_ref, wg_ref, wu_ref, o_ref):
    h = h_ref[...]
    gte = jnp.dot(h, wg_ref[...], preferred_element_type=F32)
    up = jnp.dot(h, wu_ref[...], preferred_element_type=F32)
    o_ref[...] = (jax.nn.silu(gte) * up).astype(o_ref.dtype)


def _swiglu(h2d, w_gate_up, tm=1024, tn=512):
    t = h2d.shape[0]
    ub0 = D_FF // tn
    return pl.pallas_call(
        _swiglu_kernel,
        grid=(t // tm, D_FF // tn),
        in_specs=[pl.BlockSpec((tm, D_MODEL), lambda i, j: (i, 0)),
                  pl.BlockSpec((D_MODEL, tn), lambda i, j: (0, j)),
                  pl.BlockSpec((D_MODEL, tn), lambda i, j: (0, ub0 + j))],
        out_specs=pl.BlockSpec((tm, tn), lambda i, j: (i, j)),
        out_shape=jax.ShapeDtypeStruct((t, D_FF), BF16),
        compiler_params=_params(2),
        name="swiglu",
    )(h2d, w_gate_up, w_gate_up)


def _down_kernel(a_ref, w_ref, x_ref, g_ref, *rest, with_next):
    if with_next:
        gn_ref, o_ref, h_ref, acc_ref = rest
    else:
        o_ref, acc_ref = rest
    kk = pl.program_id(1)

    @pl.when(kk == 0)
    def _init():
        acc_ref[...] = jnp.zeros(acc_ref.shape, F32)

    acc_ref[...] += jnp.dot(a_ref[...], w_ref[...], preferred_element_type=F32)

    @pl.when(kk == pl.num_programs(1) - 1)
    def _finalize():
        x = x_ref[...] + _rms(acc_ref[...], g_ref[...])
        o_ref[...] = x
        if with_next:
            h_ref[...] = _rms(x, gn_ref[...]).astype(h_ref.dtype)


def _down_proj(a, w, x2d, g, g_next=None, tm=512, tk=D_FF // 4):
    t = a.shape[0]
    with_next = g_next is not None
    row_spec = pl.BlockSpec((tm, D_MODEL), lambda i, kk: (i, 0))
    gain_spec = pl.BlockSpec((1, D_MODEL), lambda i, kk: (0, 0))
    in_specs = [pl.BlockSpec((tm, tk), lambda i, kk: (i, kk)),
                pl.BlockSpec((tk, D_MODEL), lambda i, kk: (kk, 0)),
                row_spec, gain_spec]
    args = [a, w, x2d, g.reshape(1, D_MODEL)]
    out_specs = [row_spec]
    out_shape = [jax.ShapeDtypeStruct((t, D_MODEL), F32)]
    if with_next:
        in_specs.append(gain_spec)
        args.append(g_next.reshape(1, D_MODEL))
        out_specs.append(row_spec)
        out_shape.append(jax.ShapeDtypeStruct((t, D_MODEL), BF16))
    out = pl.pallas_call(
        functools.partial(_down_kernel, with_next=with_next),
        grid=(t // tm, D_FF // tk),
        in_specs=in_specs, out_specs=out_specs, out_shape=out_shape,
        scratch_shapes=[pltpu.VMEM((tm, D_MODEL), F32)],
        compiler_params=_params(2),
        name="down_proj",
    )(*args)
    return (out[0], out[1]) if with_next else (out[0], None)


def _rope_tables(s):
    rows = s // GRID_W
    row = jnp.repeat(jnp.arange(rows, dtype=F32), GRID_W)
    col = jnp.tile(jnp.arange(GRID_W, dtype=F32), rows)
    quarter = GQA_HD // 4
    freqs = ROPE_THETA ** (-jnp.arange(quarter, dtype=F32) / quarter)
    ang_r = row[:, None] * freqs[None, :]
    ang_c = col[:, None] * freqs[None, :]
    cos = jnp.concatenate([jnp.cos(ang_r), jnp.cos(ang_c), jnp.cos(ang_r), jnp.cos(ang_c)], axis=-1)
    sin = jnp.concatenate([-jnp.sin(ang_r), -jnp.sin(ang_c), jnp.sin(ang_r), jnp.sin(ang_c)], axis=-1)
    return cos, sin


def _reorder_head_lanes(a):
    heads = a.reshape(a.shape[:-1] + (a.shape[-1] // GQA_HD, GQA_HD))
    return heads[..., jnp.asarray(_rope_lane_order())].reshape(a.shape)


def _trunk(x, weights):
    b, s, _ = x.shape
    x2d = x.reshape(b * s, D_MODEL)
    cos, sin = _rope_tables(s)
    slopes = jnp.exp2(-8.0 * jnp.arange(1, DIFF_HEADS + 1, dtype=F32) / DIFF_HEADS)
    h = _prenorm(x2d, weights[0][0])
    for l in range(DEPTH):
        (mix_pre_g, w_in, diff_lambda, diff_subln_g, gqa_qnorm_g, gqa_knorm_g, w_proj_a, w_proj_b,
         w_out, mix_post_g, ffn_pre_g, w_gate_up, w_down, ffn_post_g) = [w[l] for w in weights]
        lambda_init = 0.8 - 0.6 * math.exp(-0.3 * l)
        dqt = _proj_heads(h, w_in, b, s, OFF_DQ, DIFF_QK_WIDTH, transpose=True,
                          scale=DIFF_HD ** -0.5 * LOG2E)
        dk = _proj_heads(h, w_in, b, s, OFF_DK, DIFF_QK_WIDTH, transpose=False)
        dvt = _proj_heads(h, w_in, b, s, OFF_DV, DIFF_WIDTH, transpose=True)
        o_a = _diff_attention(dqt, dk, dvt, slopes, diff_lambda, diff_subln_g, lambda_init)
        w_rope = _reorder_head_lanes(w_in[:, OFF_GQ:OFF_GV])
        gqt = _proj_heads(h, w_rope, b, s, 0, GQA_WIDTH, transpose=True, scale=GQA_HD ** -0.5 * LOG2E,
                          rope=(_reorder_head_lanes(gqa_qnorm_g), cos, sin))
        gk = _proj_heads(h, w_rope, b, s, GQA_WIDTH, GQA_KV_WIDTH, transpose=False,
                         rope=(_reorder_head_lanes(gqa_knorm_g), cos, sin))
        gvt = _proj_heads(h, w_in, b, s, OFF_GV, GQA_KV_WIDTH, transpose=True)
        o_b = _gqa_attention(gqt, gk, gvt)
        gate = _proj_gate(h, w_in[:, OFF_GATE:])
        merged = _merge(o_a.reshape(b * s, DIFF_WIDTH), o_b.reshape(b * s, GQA_WIDTH),
                        w_proj_a, w_proj_b, gate)
        x2d, h = _out_proj(merged, w_out, x2d, mix_post_g, ffn_pre_g)
        act = _swiglu(h, w_gate_up)
        next_pre_g = weights[0][l + 1] if l + 1 < DEPTH else None
        x2d, h = _down_proj(act, w_down, x2d, ffn_post_g, next_pre_g)
    return x2d.reshape(b, s, D_MODEL)


def kernel(x_prompt, x_sample, mix_pre_g, w_in, diff_lambda, diff_subln_g, gqa_qnorm_g, gqa_knorm_g,
           w_proj_a, w_proj_b, w_out, mix_post_g, ffn_pre_g, w_gate_up, w_down, ffn_post_g):
    weights = (mix_pre_g, w_in.astype(BF16), diff_lambda, diff_subln_g, gqa_qnorm_g, gqa_knorm_g,
               w_proj_a.astype(BF16), w_proj_b.astype(BF16), w_out.astype(BF16), mix_post_g,
               ffn_pre_g, w_gate_up.astype(BF16), w_down.astype(BF16), ffn_post_g)
    return (_trunk(x_prompt, weights), _trunk(x_sample, weights))
```

```python
import functools
import math

import jax
import jax.numpy as jnp
from jax import lax
from jax.experimental import pallas as pl
from jax.experimental.pallas import tpu as pltpu

D_MODEL = 2048
DEPTH = 2
GRID_W = 64
EPS = 1e-6
DIFF_HEADS = 8
DIFF_HD = 64
DIFF_V_HD = 128
DIFF_QK_WIDTH = 1024
DIFF_WIDTH = 1024
GQA_HEADS = 8
GQA_KV_HEADS = 2
GQA_GROUP = GQA_HEADS // GQA_KV_HEADS
GQA_HD = 128
GQA_WIDTH = 1024
GQA_KV_WIDTH = 256
ROPE_THETA = 10000.0
GATE_WIDTH = 2 * D_MODEL
D_FF = 5632

HEAD_LANES = 128
OFF_DQ = 0
OFF_DK = OFF_DQ + DIFF_QK_WIDTH
OFF_DV = OFF_DK + DIFF_QK_WIDTH
OFF_GQ = OFF_DV + DIFF_WIDTH
OFF_GK = OFF_GQ + GQA_WIDTH
OFF_GV = OFF_GK + GQA_KV_WIDTH
OFF_GATE = OFF_GV + GQA_KV_WIDTH

VMEM_LIMIT_BYTES = 52 * 1024 * 1024
ATT_BLK = 512
DIFF_LOOKAHEAD = 1
DIFF_QBLOCKS = 1
GQA_LOOKAHEAD = 1
LOG2E = math.log2(math.e)

BF16 = jnp.bfloat16
F32 = jnp.float32


def _params(n_grid):
    return pltpu.CompilerParams(
        dimension_semantics=("parallel",) * (n_grid - 1) + ("arbitrary",),
        vmem_limit_bytes=VMEM_LIMIT_BYTES)


def _rms(x, g):
    ms = jnp.mean(x * x, axis=-1, keepdims=True)
    return x * lax.rsqrt(ms + EPS) * g


def _prenorm_kernel(x_ref, g_ref, o_ref):
    o_ref[...] = _rms(x_ref[...], g_ref[...]).astype(o_ref.dtype)


def _prenorm(x2d, g, tm=1024):
    t = x2d.shape[0]
    return pl.pallas_call(
        _prenorm_kernel,
        grid=(t // tm,),
        in_specs=[pl.BlockSpec((tm, D_MODEL), lambda i: (i, 0)),
                  pl.BlockSpec((1, D_MODEL), lambda i: (0, 0))],
        out_specs=pl.BlockSpec((tm, D_MODEL), lambda i: (i, 0)),
        out_shape=jax.ShapeDtypeStruct((t, D_MODEL), BF16),
        compiler_params=_params(1),
        name="prenorm",
    )(x2d, g.reshape(1, D_MODEL))


def _rope_lane_order():
    quarter = GQA_HD // 4
    return [i + off for off in (0, 2 * quarter, quarter, 3 * quarter) for i in range(quarter)]


def _rope(y, cos, sin_signed):
    return y * cos + pltpu.roll(y, GQA_HD // 2, 1) * sin_signed


def _head_mean_square(y):
    sq = (y * y).astype(BF16)
    ones = jnp.ones((HEAD_LANES, HEAD_LANES), BF16)
    return jnp.dot(sq, ones, preferred_element_type=F32) * (1.0 / HEAD_LANES)


def _store_head(o_ref, hh, y, transpose):
    if transpose:
        for c in range(y.shape[0] // ATT_BLK):
            o_ref[0, hh, c] = y[c * ATT_BLK:(c + 1) * ATT_BLK].T.astype(o_ref.dtype)
    else:
        o_ref[0, hh] = y.astype(o_ref.dtype)


def _seg_heads_kernel(h_ref, w_ref, o_ref, *, scale, transpose, n_heads):
    acc = jnp.dot(h_ref[...], w_ref[...], preferred_element_type=F32)
    for hh in range(n_heads):
        y = acc[:, hh * HEAD_LANES:(hh + 1) * HEAD_LANES]
        if scale != 1.0:
            y = y * scale
        _store_head(o_ref, hh, y, transpose)


def _seg_rope_kernel(h_ref, w_ref, g_ref, cos_ref, sin_ref, o_ref, *, scale, transpose, n_heads):
    acc = jnp.dot(h_ref[...], w_ref[...], preferred_element_type=F32)
    cos = cos_ref[...]
    sin = sin_ref[...]
    g = g_ref[...]
    for hh in range(n_heads):
        y = acc[:, hh * HEAD_LANES:(hh + 1) * HEAD_LANES]
        y = y * lax.rsqrt(_head_mean_square(y) + EPS) * g
        y = _rope(y, cos, sin)
        if scale != 1.0:
            y = y * scale
        _store_head(o_ref, hh, y, transpose)


def _seg_gate_kernel(h_ref, w_ref, o_ref):
    acc = jnp.dot(h_ref[...], w_ref[...], preferred_element_type=F32)
    o_ref[...] = jax.nn.sigmoid(acc).astype(o_ref.dtype)


def _proj_heads(h2d, w_in, b, s, col0, ncols, *, transpose, scale=1.0, rope=None, tm=1024, tn=1024):
    tn = min(tn, ncols)
    n_heads_blk = tn // HEAD_LANES
    n_heads = ncols // HEAD_LANES
    nsb = s // tm
    grid = (b * nsb, ncols // tn)
    cb0 = col0 // tn
    assert col0 % tn == 0 and s % tm == 0
    in_specs = [pl.BlockSpec((tm, D_MODEL), lambda i, j: (i, 0)),
                pl.BlockSpec((D_MODEL, tn), lambda i, j: (0, cb0 + j))]
    args = [h2d, w_in]
    if rope is None:
        body = functools.partial(_seg_heads_kernel, scale=scale, transpose=transpose, n_heads=n_heads_blk)
    else:
        gain, cos, sin = rope
        in_specs += [pl.BlockSpec((1, HEAD_LANES), lambda i, j: (0, 0)),
                     pl.BlockSpec((tm, HEAD_LANES), lambda i, j: (i % nsb, 0)),
                     pl.BlockSpec((tm, HEAD_LANES), lambda i, j: (i % nsb, 0))]
        args += [gain.reshape(1, HEAD_LANES), cos, sin]
        body = functools.partial(_seg_rope_kernel, scale=scale, transpose=transpose, n_heads=n_heads_blk)
    if transpose:
        out_shape = (b, n_heads, s // ATT_BLK, HEAD_LANES, ATT_BLK)
        out_spec = pl.BlockSpec((1, n_heads_blk, tm // ATT_BLK, HEAD_LANES, ATT_BLK),
                                lambda i, j: (i // nsb, j, i % nsb, 0, 0))
    else:
        out_shape = (b, n_heads, s, HEAD_LANES)
        out_spec = pl.BlockSpec((1, n_heads_blk, tm, HEAD_LANES), lambda i, j: (i // nsb, j, i % nsb, 0))
    return pl.pallas_call(
        body, grid=grid, in_specs=in_specs, out_specs=out_spec,
        out_shape=jax.ShapeDtypeStruct(out_shape, BF16),
        compiler_params=_params(2),
        name="proj_heads",
    )(*args)


def _proj_gate(h2d, w_gate, tm=1024, tn=1024):
    t = h2d.shape[0]
    return pl.pallas_call(
        _seg_gate_kernel,
        grid=(t // tm, GATE_WIDTH // tn),
        in_specs=[pl.BlockSpec((tm, D_MODEL), lambda i, j: (i, 0)),
                  pl.BlockSpec((D_MODEL, tn), lambda i, j: (0, j))],
        out_specs=pl.BlockSpec((tm, tn), lambda i, j: (i, j)),
        out_shape=jax.ShapeDtypeStruct((t, GATE_WIDTH), F32),
        compiler_params=_params(2),
        name="proj_gate",
    )(h2d, w_gate)


BOUNDED_STABILISER_MAX = 32.0
NORM_MARGIN = 1.02
EXP2_UNDERFLOW = 127.0
FEATURE_ROWS = 16


def _online_softmax_step(s, c, vt, m_ref, l_ref, acc_ref, idx):
    m_old = m_ref[idx]
    m_new = jnp.maximum(m_old, jnp.max(s, axis=0, keepdims=True) + c)
    p = jnp.exp2(s - (m_new - c))
    alpha = jnp.exp2(m_old - m_new)
    l_ref[idx] = alpha * l_ref[idx] + jnp.sum(p, axis=0, keepdims=True)
    acc_ref[idx] = alpha * acc_ref[idx] + jnp.dot(vt, p.astype(BF16), preferred_element_type=F32)
    m_ref[idx] = m_new


def _bounded_softmax_step(t, vt, l_ref, acc_ref, idx):
    p = jnp.exp2(t)
    l_ref[idx] = l_ref[idx] + jnp.sum(p, axis=0, keepdims=True)
    acc_ref[idx] = acc_ref[idx] + jnp.dot(vt, p.astype(BF16), preferred_element_type=F32)


def _sweep_key_blocks(lo, hi, keys, values_t, score, step, n_chains, s_s, unroll=1,
                      block_of=lambda t: t, after_prime=None):
    ahead = s_s.shape[0]
    j_first = block_of(lo)
    k_first = keys(j_first)
    for i in range(ahead):
        s_s[i] = score(k_first, j_first, i)
    if after_prime is not None:
        after_prime()

    def key_blocks(t0, count):
        queue = [s_s[i] for i in range(ahead)]
        for dt in range(count):
            j = block_of(t0 + dt)
            j_next = block_of(jnp.minimum(t0 + dt + 1, hi - 1))
            k = keys(j)
            k_next = keys(j_next)
            vt = values_t(j)
            for ci in range(n_chains):
                nxt = ci + ahead
                queue.append(score(k, j, nxt) if nxt < n_chains else score(k_next, j_next, nxt - n_chains))
                step(queue.pop(0), ci, j, vt)
        for i in range(ahead):
            s_s[i] = queue[i]

    n_groups = (hi - lo) // unroll

    def group(i, carry):
        key_blocks(lo + i * unroll, unroll)
        return carry

    lax.fori_loop(0, n_groups, group, 0)
    if unroll > 1 and not (isinstance(n_groups, int) and n_groups * unroll == hi - lo):
        def single(j, carry):
            key_blocks(j, 1)
            return carry

        lax.fori_loop(lo + n_groups * unroll, hi, single, 0)


def _split_bf16x3(x):
    p0 = x.astype(BF16).astype(F32)
    p1 = (x - p0).astype(BF16).astype(F32)
    p2 = (x - p0 - p1).astype(BF16).astype(F32)
    return p0, p1, p2


def _column_norms(x_t):
    x = x_t.astype(F32)
    return jnp.sqrt(jnp.sum(x * x, axis=0, keepdims=True))


def _diff_attn_kernel(slopes_ref, windows_ref, lam_ref, g_ref, qt_ref, k_ref, vt_ref, o_ref,
                      q_s, bias_s, feat_s, s_s, u_s, m_s, l_s, acc_s, kmax_s, *, lambda_init, n_kblk):
    blk = ATT_BLK
    n_chains = 2 * DIFF_QBLOCKS
    h = pl.program_id(1)
    qi = pl.program_id(2)
    q0 = qi * DIFF_QBLOCKS
    slope = slopes_ref[h] * LOG2E
    half = DIFF_HD

    def keys(j):
        return k_ref[0, 0, pl.ds(pl.multiple_of(j * blk, blk), blk), :]

    @pl.when(qi == 0)
    def _head_init():
        rel = (lax.broadcasted_iota(jnp.int32, (blk, blk), 0)
               - lax.broadcasted_iota(jnp.int32, (blk, blk), 1)).astype(F32)
        bias_s[0] = slope * rel
        bias_s[1] = -slope * jnp.abs(rel)
        bias_s[2] = -slope * rel

        lane = lax.broadcasted_iota(jnp.int32, (blk, HEAD_LANES), 1)
        key_offset = lax.broadcasted_iota(jnp.int32, (blk, HEAD_LANES), 0).astype(F32)
        for side, sign in ((0, 1.0), (1, -1.0)):
            a0, a1, a2 = _split_bf16x3((sign * slope) * key_offset)
            feat_s[side] = jnp.where(lane == 0, a0, jnp.where(lane == 1, a1, jnp.where(lane == 2, a2,
                                     jnp.where(lane < 6, 1.0, 0.0)))).astype(BF16)

        first_map = lax.broadcasted_iota(jnp.int32, (blk, HEAD_LANES), 1) < half

        def norm_block(j, best):
            kb = keys(j).astype(F32)
            sq = kb * kb
            n0 = jnp.sqrt(jnp.sum(jnp.where(first_map, sq, 0.0), axis=1, keepdims=True))
            n1 = jnp.sqrt(jnp.sum(jnp.where(first_map, 0.0, sq), axis=1, keepdims=True))
            return jnp.maximum(best[0], jnp.max(n0)), jnp.maximum(best[1], jnp.max(n1))

        kmax = lax.fori_loop(0, n_kblk, norm_block, (jnp.float32(0.0), jnp.float32(0.0)))
        kmax_s[0] = kmax[0]
        kmax_s[1] = kmax[1]

    zeros = jnp.zeros((half, blk), BF16)
    for qb in range(DIFF_QBLOCKS):
        q_s[2 * qb, 0:half, :] = qt_ref[0, 0, qb, 0:half, :]
        q_s[2 * qb, half:, :] = zeros
        q_s[2 * qb + 1, 0:half, :] = zeros
        q_s[2 * qb + 1, half:, :] = qt_ref[0, 0, qb, half:, :]
    m_s[...] = jnp.full(m_s.shape, -jnp.inf, F32)
    l_s[...] = jnp.zeros(l_s.shape, F32)
    acc_s[...] = jnp.zeros(acc_s.shape, F32)

    u_max = jnp.float32(0.0)
    for ci in range(n_chains):
        u = _column_norms(q_s[ci]) * (kmax_s[ci % 2] * NORM_MARGIN)
        u_s[ci] = u
        u_max = jnp.maximum(u_max, jnp.max(u))
    bounded = u_max <= BOUNDED_STABILISER_MAX

    def score(kblk, j, ci):
        return jnp.dot(kblk, q_s[ci], preferred_element_type=F32)

    slope_row = jnp.full((1, blk), slopes_ref[h], F32) * LOG2E

    def origin_gap(j, qblk):
        return jnp.full((1, blk), jnp.abs(j - qblk) * blk, jnp.int32).astype(F32)

    @pl.when(bounded)
    def _bounded_sweep():
        w = windows_ref[h]
        lo = jnp.maximum(q0 - w, 0)
        hi = jnp.minimum(q0 + 1 + w, n_kblk)

        def block_of(t):
            j = lo + t
            return jnp.minimum(j + (j >= q0).astype(jnp.int32), n_kblk - 1)

        query_offset = lax.broadcasted_iota(jnp.int32, (1, blk), 1).astype(F32)
        feature_row = lax.broadcasted_iota(jnp.int32, (FEATURE_ROWS, blk), 0)
        rhs_pad = jnp.zeros((HEAD_LANES - FEATURE_ROWS, blk), BF16)

        def feature_score(kblk, j, ci):
            right = j > q0
            signed_slope = jnp.where(right, -slope_row, slope_row)
            row = -signed_slope * query_offset - slope_row * origin_gap(j, q0) - u_s[ci]
            r0, r1, r2 = _split_bf16x3(row)
            feats = jnp.where(feature_row < 3, 1.0,
                              jnp.where(feature_row == 3, r0,
                                        jnp.where(feature_row == 4, r1,
                                                  jnp.where(feature_row == 5, r2, 0.0))))
            rhs = jnp.concatenate([q_s[ci], feats.astype(BF16), rhs_pad], axis=0)
            lhs = jnp.concatenate([kblk, feat_s[right.astype(jnp.int32)]], axis=1)
            return jnp.dot(lhs, rhs, preferred_element_type=F32)

        def own_block():
            kd = keys(q0)
            vt = vt_ref[0, 0, q0]
            for ci in range(n_chains):
                s = jnp.dot(kd, q_s[ci], preferred_element_type=F32)
                _bounded_softmax_step(s + bias_s[1] - u_s[ci], vt, l_s, acc_s, ci)

        def step(s, ci, j, vt):
            _bounded_softmax_step(s, vt, l_s, acc_s, ci)

        _sweep_key_blocks(0, hi - lo - 1, keys, lambda j: vt_ref[0, 0, j], feature_score, step, n_chains, s_s,
                          unroll=2, block_of=block_of, after_prime=own_block)

    @pl.when(jnp.logical_not(bounded))
    def _online_sweep():
        def step(s, ci, j, vt):
            side = jnp.where(j < q0, 0, jnp.where(j == q0, 1, 2))
            _online_softmax_step(s + bias_s[side], -slope_row * origin_gap(j, q0), vt, m_s, l_s, acc_s, ci)

        _sweep_key_blocks(0, n_kblk, keys, lambda j: vt_ref[0, 0, j], score, step, n_chains, s_s)

    dl = lam_ref[...]
    lam = (jnp.exp(jnp.sum(dl[0:1] * dl[1:2], axis=-1, keepdims=True))
           - jnp.exp(jnp.sum(dl[2:3] * dl[3:4], axis=-1, keepdims=True)) + lambda_init)
    for qb in range(DIFF_QBLOCKS):
        c0, c1 = 2 * qb, 2 * qb + 1
        o = acc_s[c0] * (1.0 / l_s[c0]) - lam * (acc_s[c1] * (1.0 / l_s[c1]))
        ms = jnp.mean(o * o, axis=0, keepdims=True)
        y = o * lax.rsqrt(ms + EPS) * g_ref[...] * (1.0 - lambda_init)
        o_ref[0, qb * blk:(qb + 1) * blk, :] = y.T.astype(o_ref.dtype)


def _alibi_windows(slopes, blk):
    dist = EXP2_UNDERFLOW / (slopes * LOG2E)
    return (jnp.floor((dist - 1.0) / blk) + 1.0).astype(jnp.int32)


def _diff_attention(qt, k, vt, slopes, diff_lambda, subln_g, lambda_init):
    b, nh, nb, _, blk = qt.shape
    s = nb * blk
    body = functools.partial(_diff_attn_kernel, lambda_init=lambda_init, n_kblk=nb)
    qbs = DIFF_QBLOCKS
    assert qbs == 1, "the bounded sweep folds the bias of one query block into the score matmul"
    n_chains = 2 * qbs
    return pl.pallas_call(
        body,
        grid=(b, nh, nb // qbs),
        in_specs=[pl.BlockSpec(memory_space=pltpu.SMEM),
                  pl.BlockSpec(memory_space=pltpu.SMEM),
                  pl.BlockSpec((4, DIFF_HD), lambda bb, hh, qi: (0, 0)),
                  pl.BlockSpec((DIFF_V_HD, 1), lambda bb, hh, qi: (0, 0)),
                  pl.BlockSpec((1, 1, qbs, HEAD_LANES, blk), lambda bb, hh, qi: (bb, hh, qi, 0, 0)),
                  pl.BlockSpec((1, 1, s, HEAD_LANES), lambda bb, hh, qi: (bb, hh, 0, 0)),
                  pl.BlockSpec((1, 1, nb, HEAD_LANES, blk), lambda bb, hh, qi: (bb, hh, 0, 0, 0))],
        out_specs=pl.BlockSpec((1, qbs * blk, HEAD_LANES), lambda bb, hh, qi: (bb, qi, hh)),
        out_shape=jax.ShapeDtypeStruct((b, s, nh * HEAD_LANES), BF16),
        scratch_shapes=[pltpu.VMEM((n_chains, HEAD_LANES, blk), BF16),
                        pltpu.VMEM((3, blk, blk), F32),
                        pltpu.VMEM((2, blk, HEAD_LANES), BF16),
                        pltpu.VMEM((DIFF_LOOKAHEAD, blk, blk), F32),
                        pltpu.VMEM((n_chains, 1, blk), F32),
                        pltpu.VMEM((n_chains, 1, blk), F32),
                        pltpu.VMEM((n_chains, 1, blk), F32),
                        pltpu.VMEM((n_chains, HEAD_LANES, blk), F32),
                        pltpu.SMEM((2,), F32)],
        compiler_params=pltpu.CompilerParams(dimension_semantics=("arbitrary",) * 3,
                                             vmem_limit_bytes=VMEM_LIMIT_BYTES),
        name="diff_attention",
    )(slopes, _alibi_windows(slopes, blk), diff_lambda, subln_g.reshape(DIFF_V_HD, 1), qt, k, vt)


def _gqa_attn_kernel(qt_ref, k_ref, vt_ref, o_ref, s_s, u_s, m_s, l_s, acc_s, kmax_s, *, n_kblk):
    blk = ATT_BLK
    qi = pl.program_id(2)

    def keys(j):
        return k_ref[0, 0, pl.ds(pl.multiple_of(j * blk, blk), blk), :]

    @pl.when(qi == 0)
    def _kv_head_init():
        def norm_block(j, best):
            kb = keys(j).astype(F32)
            return jnp.maximum(best, jnp.max(jnp.sqrt(jnp.sum(kb * kb, axis=1, keepdims=True))))

        kmax_s[0] = lax.fori_loop(0, n_kblk, norm_block, jnp.float32(0.0))

    m_s[...] = jnp.full(m_s.shape, -jnp.inf, F32)
    l_s[...] = jnp.zeros(l_s.shape, F32)
    acc_s[...] = jnp.zeros(acc_s.shape, F32)

    u_max = jnp.float32(0.0)
    for g in range(GQA_GROUP):
        u = _column_norms(qt_ref[0, g, 0]) * (kmax_s[0] * NORM_MARGIN)
        u_s[g] = u
        u_max = jnp.maximum(u_max, jnp.max(u))
    bounded = u_max <= BOUNDED_STABILISER_MAX

    def score(kblk, j, g):
        return jnp.dot(kblk, qt_ref[0, g, 0], preferred_element_type=F32)

    @pl.when(bounded)
    def _bounded_sweep():
        def step(s, g, j, vt):
            _bounded_softmax_step(s - u_s[g], vt, l_s, acc_s, g)

        _sweep_key_blocks(0, n_kblk, keys, lambda j: vt_ref[0, 0, j], score, step, GQA_GROUP, s_s, unroll=4)

    @pl.when(jnp.logical_not(bounded))
    def _online_sweep():
        def step(s, g, j, vt):
            _online_softmax_step(s, 0.0, vt, m_s, l_s, acc_s, g)

        _sweep_key_blocks(0, n_kblk, keys, lambda j: vt_ref[0, 0, j], score, step, GQA_GROUP, s_s)

    for g in range(GQA_GROUP):
        o = acc_s[g] * (1.0 / l_s[g])
        o_ref[0, :, g * HEAD_LANES:(g + 1) * HEAD_LANES] = o.T.astype(o_ref.dtype)


def _gqa_attention(qt, k, vt):
    b, nh, nb, _, blk = qt.shape
    s = nb * blk
    nkv = nh // GQA_GROUP
    body = functools.partial(_gqa_attn_kernel, n_kblk=nb)
    return pl.pallas_call(
        body,
        grid=(b, nkv, nb),
        in_specs=[pl.BlockSpec((1, GQA_GROUP, 1, HEAD_LANES, blk), lambda bb, kv, qi: (bb, kv, qi, 0, 0)),
                  pl.BlockSpec((1, 1, s, HEAD_LANES), lambda bb, kv, qi: (bb, kv, 0, 0)),
                  pl.BlockSpec((1, 1, nb, HEAD_LANES, blk), lambda bb, kv, qi: (bb, kv, 0, 0, 0))],
        out_specs=pl.BlockSpec((1, blk, GQA_GROUP * HEAD_LANES), lambda bb, kv, qi: (bb, qi, kv)),
        out_shape=jax.ShapeDtypeStruct((b, s, nh * HEAD_LANES), BF16),
        scratch_shapes=[pltpu.VMEM((GQA_LOOKAHEAD, blk, blk), F32),
                        pltpu.VMEM((GQA_GROUP, 1, blk), F32),
                        pltpu.VMEM((GQA_GROUP, 1, blk), F32),
                        pltpu.VMEM((GQA_GROUP, 1, blk), F32),
                        pltpu.VMEM((GQA_GROUP, HEAD_LANES, blk), F32),
                        pltpu.SMEM((1,), F32)],
        compiler_params=pltpu.CompilerParams(dimension_semantics=("arbitrary",) * 3,
                                             vmem_limit_bytes=VMEM_LIMIT_BYTES),
        name="gqa_attention",
    )(qt, k, vt)


def _merge_kernel(oa_ref, ob_ref, wa_ref, wb_ref, ga_ref, gb_ref, o_ref):
    a = jnp.dot(oa_ref[...], wa_ref[...], preferred_element_type=F32)
    bb = jnp.dot(ob_ref[...], wb_ref[...], preferred_element_type=F32)
    o_ref[...] = (ga_ref[...] * a + gb_ref[...] * bb).astype(o_ref.dtype)


def _merge(oa, ob, wa, wb, gate, tm=1024, tn=1024):
    t = oa.shape[0]
    gb0 = D_MODEL // tn
    return pl.pallas_call(
        _merge_kernel,
        grid=(t // tm, D_MODEL // tn),
        in_specs=[pl.BlockSpec((tm, DIFF_WIDTH), lambda i, j: (i, 0)),
                  pl.BlockSpec((tm, GQA_WIDTH), lambda i, j: (i, 0)),
                  pl.BlockSpec((DIFF_WIDTH, tn), lambda i, j: (0, j)),
                  pl.BlockSpec((GQA_WIDTH, tn), lambda i, j: (0, j)),
                  pl.BlockSpec((tm, tn), lambda i, j: (i, j)),
                  pl.BlockSpec((tm, tn), lambda i, j: (i, gb0 + j))],
        out_specs=pl.BlockSpec((tm, tn), lambda i, j: (i, j)),
        out_shape=jax.ShapeDtypeStruct((t, D_MODEL), BF16),
        compiler_params=_params(2),
        name="merge",
    )(oa, ob, wa, wb, gate, gate)


def _out_proj_kernel(a_ref, w_ref, x_ref, g_ref, gn_ref, o_ref, h_ref):
    m = jnp.dot(a_ref[...], w_ref[...], preferred_element_type=F32)
    x = x_ref[...] + _rms(m, g_ref[...])
    o_ref[...] = x
    h_ref[...] = _rms(x, gn_ref[...]).astype(h_ref.dtype)


def _out_proj(a, w, x2d, g, g_next, tm=512):
    t = a.shape[0]
    return pl.pallas_call(
        _out_proj_kernel,
        grid=(t // tm,),
        in_specs=[pl.BlockSpec((tm, D_MODEL), lambda i: (i, 0)),
                  pl.BlockSpec((D_MODEL, D_MODEL), lambda i: (0, 0)),
                  pl.BlockSpec((tm, D_MODEL), lambda i: (i, 0)),
                  pl.BlockSpec((1, D_MODEL), lambda i: (0, 0)),
                  pl.BlockSpec((1, D_MODEL), lambda i: (0, 0))],
        out_specs=[pl.BlockSpec((tm, D_MODEL), lambda i: (i, 0)),
                   pl.BlockSpec((tm, D_MODEL), lambda i: (i, 0))],
        out_shape=[jax.ShapeDtypeStruct((t, D_MODEL), F32),
                   jax.ShapeDtypeStruct((t, D_MODEL), BF16)],
        compiler_params=_params(1),
        name="out_proj",
    )(a, w, x2d, g.reshape(1, D_MODEL), g_next.reshape(1, D_MODEL))


def _swiglu_kernel(h_ref, wg_ref, wu_ref, o_ref):
    h = h_ref[...]
    gte = jnp.dot(h, wg_ref[...], preferred_element_type=F32)
    up = jnp.dot(h, wu_ref[...], preferred_element_type=F32)
    o_ref[...] = (jax.nn.silu(gte) * up).astype(o_ref.dtype)


def _swiglu(h2d, w_gate_up, tm=1024, tn=512):
    t = h2d.shape[0]
    ub0 = D_FF // tn
    return pl.pallas_call(
        _swiglu_kernel,
        grid=(t // tm, D_FF // tn),
        in_specs=[pl.BlockSpec((tm, D_MODEL), lambda i, j: (i, 0)),
                  pl.BlockSpec((D_MODEL, tn), lambda i, j: (0, j)),
                  pl.BlockSpec((D_MODEL, tn), lambda i, j: (0, ub0 + j))],
        out_specs=pl.BlockSpec((tm, tn), lambda i, j: (i, j)),
        out_shape=jax.ShapeDtypeStruct((t, D_FF), BF16),
        compiler_params=_params(2),
        name="swiglu",
    )(h2d, w_gate_up, w_gate_up)


def _down_kernel(a_ref, w_ref, x_ref, g_ref, *rest, with_next):
    if with_next:
        gn_ref, o_ref, h_ref, acc_ref = rest
    else:
        o_ref, acc_ref = rest
    kk = pl.program_id(1)

    @pl.when(kk == 0)
    def _init():
        acc_ref[...] = jnp.zeros(acc_ref.shape, F32)

    acc_ref[...] += jnp.dot(a_ref[...], w_ref[...], preferred_element_type=F32)

    @pl.when(kk == pl.num_programs(1) - 1)
    def _finalize():
        x = x_ref[...] + _rms(acc_ref[...], g_ref[...])
        o_ref[...] = x
        if with_next:
            h_ref[...] = _rms(x, gn_ref[...]).astype(h_ref.dtype)


def _down_proj(a, w, x2d, g, g_next=None, tm=512, tk=D_FF // 4):
    t = a.shape[0]
    with_next = g_next is not None
    row_spec = pl.BlockSpec((tm, D_MODEL), lambda i, kk: (i, 0))
    gain_spec = pl.BlockSpec((1, D_MODEL), lambda i, kk: (0, 0))
    in_specs = [pl.BlockSpec((tm, tk), lambda i, kk: (i, kk)),
                pl.BlockSpec((tk, D_MODEL), lambda i, kk: (kk, 0)),
                row_spec, gain_spec]
    args = [a, w, x2d, g.reshape(1, D_MODEL)]
    out_specs = [row_spec]
    out_shape = [jax.ShapeDtypeStruct((t, D_MODEL), F32)]
    if with_next:
        in_specs.append(gain_spec)
        args.append(g_next.reshape(1, D_MODEL))
        out_specs.append(row_spec)
        out_shape.append(jax.ShapeDtypeStruct((t, D_MODEL), BF16))
    out = pl.pallas_call(
        functools.partial(_down_kernel, with_next=with_next),
        grid=(t // tm, D_FF // tk),
        in_specs=in_specs, out_specs=out_specs, out_shape=out_shape,
        scratch_shapes=[pltpu.VMEM((tm, D_MODEL), F32)],
        compiler_params=_params(2),
        name="down_proj",
    )(*args)
    return (out[0], out[1]) if with_next else (out[0], None)


def _rope_tables(s):
    rows = s // GRID_W
    row = jnp.repeat(jnp.arange(rows, dtype=F32), GRID_W)
    col = jnp.tile(jnp.arange(GRID_W, dtype=F32), rows)
    quarter = GQA_HD // 4
    freqs = ROPE_THETA ** (-jnp.arange(quarter, dtype=F32) / quarter)
    ang_r = row[:, None] * freqs[None, :]
    ang_c = col[:, None] * freqs[None, :]
    cos = jnp.concatenate([jnp.cos(ang_r), jnp.cos(ang_c), jnp.cos(ang_r), jnp.cos(ang_c)], axis=-1)
    sin = jnp.concatenate([-jnp.sin(ang_r), -jnp.sin(ang_c), jnp.sin(ang_r), jnp.sin(ang_c)], axis=-1)
    return cos, sin


def _reorder_head_lanes(a):
    heads = a.reshape(a.shape[:-1] + (a.shape[-1] // GQA_HD, GQA_HD))
    return heads[..., jnp.asarray(_rope_lane_order())].reshape(a.shape)


def _trunk(x, weights):
    b, s, _ = x.shape
    x2d = x.reshape(b * s, D_MODEL)
    cos, sin = _rope_tables(s)
    slopes = jnp.exp2(-8.0 * jnp.arange(1, DIFF_HEADS + 1, dtype=F32) / DIFF_HEADS)
    h = _prenorm(x2d, weights[0][0])
    for l in range(DEPTH):
        (mix_pre_g, w_in, diff_lambda, diff_subln_g, gqa_qnorm_g, gqa_knorm_g, w_proj_a, w_proj_b,
         w_out, mix_post_g, ffn_pre_g, w_gate_up, w_down, ffn_post_g) = [w[l] for w in weights]
        lambda_init = 0.8 - 0.6 * math.exp(-0.3 * l)
        dqt = _proj_heads(h, w_in, b, s, OFF_DQ, DIFF_QK_WIDTH, transpose=True,
                          scale=DIFF_HD ** -0.5 * LOG2E)
        dk = _proj_heads(h, w_in, b, s, OFF_DK, DIFF_QK_WIDTH, transpose=False)
        dvt = _proj_heads(h, w_in, b, s, OFF_DV, DIFF_WIDTH, transpose=True)
        o_a = _diff_attention(dqt, dk, dvt, slopes, diff_lambda, diff_subln_g, lambda_init)
        w_rope = _reorder_head_lanes(w_in[:, OFF_GQ:OFF_GV])
        gqt = _proj_heads(h, w_rope, b, s, 0, GQA_WIDTH, transpose=True, scale=GQA_HD ** -0.5 * LOG2E,
                          rope=(_reorder_head_lanes(gqa_qnorm_g), cos, sin))
        gk = _proj_heads(h, w_rope, b, s, GQA_WIDTH, GQA_KV_WIDTH, transpose=False,
                         rope=(_reorder_head_lanes(gqa_knorm_g), cos, sin))
        gvt = _proj_heads(h, w_in, b, s, OFF_GV, GQA_KV_WIDTH, transpose=True)
        o_b = _gqa_attention(gqt, gk, gvt)
        gate = _proj_gate(h, w_in[:, OFF_GATE:])
        merged = _merge(o_a.reshape(b * s, DIFF_WIDTH), o_b.reshape(b * s, GQA_WIDTH),
                        w_proj_a, w_proj_b, gate)
        x2d, h = _out_proj(merged, w_out, x2d, mix_post_g, ffn_pre_g)
        act = _swiglu(h, w_gate_up)
        next_pre_g = weights[0][l + 1] if l + 1 < DEPTH else None
        x2d, h = _down_proj(act, w_down, x2d, ffn_post_g, next_pre_g)
    return x2d.reshape(b, s, D_MODEL)


def kernel(x_prompt, x_sample, mix_pre_g, w_in, diff_lambda, diff_subln_g, gqa_qnorm_g, gqa_knorm_g,
           w_proj_a, w_proj_b, w_out, mix_post_g, ffn_pre_g, w_gate_up, w_down, ffn_post_g):
    weights = (mix_pre_g, w_in.astype(BF16), diff_lambda, diff_subln_g, gqa_qnorm_g, gqa_knorm_g,
               w_proj_a.astype(BF16), w_proj_b.astype(BF16), w_out.astype(BF16), mix_post_g,
               ffn_pre_g, w_gate_up.astype(BF16), w_down.astype(BF16), ffn_post_g)
    return (_trunk(x_prompt, weights), _trunk(x_sample, weights))
```

```python
import functools
import math

import jax
import jax.numpy as jnp
from jax import lax
from jax.experimental import pallas as pl
from jax.experimental.pallas import tpu as pltpu

D_MODEL = 2048
DEPTH = 2
GRID_W = 64
EPS = 1e-6
DIFF_HEADS = 8
DIFF_HD = 64
DIFF_V_HD = 128
DIFF_QK_WIDTH = 1024
DIFF_WIDTH = 1024
GQA_HEADS = 8
GQA_KV_HEADS = 2
GQA_GROUP = GQA_HEADS // GQA_KV_HEADS
GQA_HD = 128
GQA_WIDTH = 1024
GQA_KV_WIDTH = 256
ROPE_THETA = 10000.0
GATE_WIDTH = 2 * D_MODEL
D_FF = 5632

HEAD_LANES = 128
OFF_DQ = 0
OFF_DK = OFF_DQ + DIFF_QK_WIDTH
OFF_DV = OFF_DK + DIFF_QK_WIDTH
OFF_GQ = OFF_DV + DIFF_WIDTH
OFF_GK = OFF_GQ + GQA_WIDTH
OFF_GV = OFF_GK + GQA_KV_WIDTH
OFF_GATE = OFF_GV + GQA_KV_WIDTH

VMEM_LIMIT_BYTES = 52 * 1024 * 1024
ATT_BLK = 512
DIFF_LOOKAHEAD = 1
DIFF_QBLOCKS = 1
GQA_LOOKAHEAD = 1
LOG2E = math.log2(math.e)

BF16 = jnp.bfloat16
F32 = jnp.float32


def _params(n_grid):
    return pltpu.CompilerParams(
        dimension_semantics=("parallel",) * (n_grid - 1) + ("arbitrary",),
        vmem_limit_bytes=VMEM_LIMIT_BYTES)


def _rms(x, g):
    ms = jnp.mean(x * x, axis=-1, keepdims=True)
    return x * lax.rsqrt(ms + EPS) * g


def _prenorm_kernel(x_ref, g_ref, o_ref):
    o_ref[...] = _rms(x_ref[...], g_ref[...]).astype(o_ref.dtype)


def _prenorm(x2d, g, tm=1024):
    t = x2d.shape[0]
    return pl.pallas_call(
        _prenorm_kernel,
        grid=(t // tm,),
        in_specs=[pl.BlockSpec((tm, D_MODEL), lambda i: (i, 0)),
                  pl.BlockSpec((1, D_MODEL), lambda i: (0, 0))],
        out_specs=pl.BlockSpec((tm, D_MODEL), lambda i: (i, 0)),
        out_shape=jax.ShapeDtypeStruct((t, D_MODEL), BF16),
        compiler_params=_params(1),
        name="prenorm",
    )(x2d, g.reshape(1, D_MODEL))


def _rope_lane_order():
    quarter = GQA_HD // 4
    return [i + off for off in (0, 2 * quarter, quarter, 3 * quarter) for i in range(quarter)]


def _rope(y, cos, sin_signed):
    return y * cos + pltpu.roll(y, GQA_HD // 2, 1) * sin_signed


def _head_mean_square(y):
    sq = (y * y).astype(BF16)
    ones = jnp.ones((HEAD_LANES, HEAD_LANES), BF16)
    return jnp.dot(sq, ones, preferred_element_type=F32) * (1.0 / HEAD_LANES)


def _store_head(o_ref, hh, y, transpose):
    if transpose:
        for c in range(y.shape[0] // ATT_BLK):
            o_ref[0, hh, c] = y[c * ATT_BLK:(c + 1) * ATT_BLK].T.astype(o_ref.dtype)
    else:
        o_ref[0, hh] = y.astype(o_ref.dtype)


def _seg_heads_kernel(h_ref, w_ref, o_ref, *, scale, transpose, n_heads):
    acc = jnp.dot(h_ref[...], w_ref[...], preferred_element_type=F32)
    for hh in range(n_heads):
        y = acc[:, hh * HEAD_LANES:(hh + 1) * HEAD_LANES]
        if scale != 1.0:
            y = y * scale
        _store_head(o_ref, hh, y, transpose)


def _seg_rope_kernel(h_ref, w_ref, g_ref, cos_ref, sin_ref, o_ref, *, scale, transpose, n_heads):
    acc = jnp.dot(h_ref[...], w_ref[...], preferred_element_type=F32)
    cos = cos_ref[...]
    sin = sin_ref[...]
    g = g_ref[...]
    for hh in range(n_heads):
        y = acc[:, hh * HEAD_LANES:(hh + 1) * HEAD_LANES]
        y = y * lax.rsqrt(_head_mean_square(y) + EPS) * g
        y = _rope(y, cos, sin)
        if scale != 1.0:
            y = y * scale
        _store_head(o_ref, hh, y, transpose)


def _seg_gate_kernel(h_ref, w_ref, o_ref):
    acc = jnp.dot(h_ref[...], w_ref[...], preferred_element_type=F32)
    o_ref[...] = jax.nn.sigmoid(acc).astype(o_ref.dtype)


def _proj_heads(h2d, w_in, b, s, col0, ncols, *, transpose, scale=1.0, rope=None, tm=1024, tn=1024):
    tn = min(tn, ncols)
    n_heads_blk = tn // HEAD_LANES
    n_heads = ncols // HEAD_LANES
    nsb = s // tm
    grid = (b * nsb, ncols // tn)
    cb0 = col0 // tn
    assert col0 % tn == 0 and s % tm == 0
    in_specs = [pl.BlockSpec((tm, D_MODEL), lambda i, j: (i, 0)),
                pl.BlockSpec((D_MODEL, tn), lambda i, j: (0, cb0 + j))]
    args = [h2d, w_in]
    if rope is None:
        body = functools.partial(_seg_heads_kernel, scale=scale, transpose=transpose, n_heads=n_heads_blk)
    else:
        gain, cos, sin = rope
        in_specs += [pl.BlockSpec((1, HEAD_LANES), lambda i, j: (0, 0)),
                     pl.BlockSpec((tm, HEAD_LANES), lambda i, j: (i % nsb, 0)),
                     pl.BlockSpec((tm, HEAD_LANES), lambda i, j: (i % nsb, 0))]
        args += [gain.reshape(1, HEAD_LANES), cos, sin]
        body = functools.partial(_seg_rope_kernel, scale=scale, transpose=transpose, n_heads=n_heads_blk)
    if transpose:
        out_shape = (b, n_heads, s // ATT_BLK, HEAD_LANES, ATT_BLK)
        out_spec = pl.BlockSpec((1, n_heads_blk, tm // ATT_BLK, HEAD_LANES, ATT_BLK),
                                lambda i, j: (i // nsb, j, i % nsb, 0, 0))
    else:
        out_shape = (b, n_heads, s, HEAD_LANES)
        out_spec = pl.BlockSpec((1, n_heads_blk, tm, HEAD_LANES), lambda i, j: (i // nsb, j, i % nsb, 0))
    return pl.pallas_call(
        body, grid=grid, in_specs=in_specs, out_specs=out_spec,
        out_shape=jax.ShapeDtypeStruct(out_shape, BF16),
        compiler_params=_params(2),
        name="proj_heads",
    )(*args)


def _proj_gate(h2d, w_gate, tm=1024, tn=1024):
    t = h2d.shape[0]
    return pl.pallas_call(
        _seg_gate_kernel,
        grid=(t // tm, GATE_WIDTH // tn),
        in_specs=[pl.BlockSpec((tm, D_MODEL), lambda i, j: (i, 0)),
                  pl.BlockSpec((D_MODEL, tn), lambda i, j: (0, j))],
        out_specs=pl.BlockSpec((tm, tn), lambda i, j: (i, j)),
        out_shape=jax.ShapeDtypeStruct((t, GATE_WIDTH), F32),
        compiler_params=_params(2),
        name="proj_gate",
    )(h2d, w_gate)


BOUNDED_STABILISER_MAX = 32.0
NORM_MARGIN = 1.02
EXP2_UNDERFLOW = 127.0
FEATURE_ROWS = 16


def _online_softmax_step(s, c, vt, m_ref, l_ref, acc_ref, idx):
    m_old = m_ref[idx]
    m_new = jnp.maximum(m_old, jnp.max(s, axis=0, keepdims=True) + c)
    p = jnp.exp2(s - (m_new - c))
    alpha = jnp.exp2(m_old - m_new)
    l_ref[idx] = alpha * l_ref[idx] + jnp.sum(p, axis=0, keepdims=True)
    acc_ref[idx] = alpha * acc_ref[idx] + jnp.dot(vt, p.astype(BF16), preferred_element_type=F32)
    m_ref[idx] = m_new


def _bounded_softmax_step(t, vt, l_ref, acc_ref, idx):
    p = jnp.exp2(t)
    l_ref[idx] = l_ref[idx] + jnp.sum(p, axis=0, keepdims=True)
    acc_ref[idx] = acc_ref[idx] + jnp.dot(vt, p.astype(BF16), preferred_element_type=F32)


def _sweep_key_blocks(lo, hi, keys, values_t, score, step, n_chains, s_s, unroll=1,
                      block_of=lambda t: t, after_prime=None):
    ahead = s_s.shape[0]
    j_first = block_of(lo)
    k_first = keys(j_first)
    for i in range(ahead):
        s_s[i] = score(k_first, j_first, i)
    if after_prime is not None:
        after_prime()

    def key_blocks(t0, count):
        queue = [s_s[i] for i in range(ahead)]
        for dt in range(count):
            j = block_of(t0 + dt)
            j_next = block_of(jnp.minimum(t0 + dt + 1, hi - 1))
            k = keys(j)
            k_next = keys(j_next)
            vt = values_t(j)
            for ci in range(n_chains):
                nxt = ci + ahead
                queue.append(score(k, j, nxt) if nxt < n_chains else score(k_next, j_next, nxt - n_chains))
                step(queue.pop(0), ci, j, vt)
        for i in range(ahead):
            s_s[i] = queue[i]

    n_groups = (hi - lo) // unroll

    def group(i, carry):
        key_blocks(lo + i * unroll, unroll)
        return carry

    lax.fori_loop(0, n_groups, group, 0)
    if unroll > 1 and not (isinstance(n_groups, int) and n_groups * unroll == hi - lo):
        def single(j, carry):
            key_blocks(j, 1)
            return carry

        lax.fori_loop(lo + n_groups * unroll, hi, single, 0)


def _split_bf16x3(x):
    p0 = x.astype(BF16).astype(F32)
    p1 = (x - p0).astype(BF16).astype(F32)
    p2 = (x - p0 - p1).astype(BF16).astype(F32)
    return p0, p1, p2


def _column_norms(x_t):
    x = x_t.astype(F32)
    return jnp.sqrt(jnp.sum(x * x, axis=0, keepdims=True))


def _diff_attn_kernel(slopes_ref, windows_ref, lam_ref, g_ref, qt_ref, k_ref, vt_ref, o_ref,
                      q_s, bias_s, feat_s, s_s, u_s, m_s, l_s, acc_s, kmax_s, *, lambda_init, n_kblk):
    blk = ATT_BLK
    n_chains = 2 * DIFF_QBLOCKS
    h = pl.program_id(1)
    qi = pl.program_id(2)
    q0 = qi * DIFF_QBLOCKS
    slope = slopes_ref[h] * LOG2E
    half = DIFF_HD

    def keys(j):
        return k_ref[0, 0, pl.ds(pl.multiple_of(j * blk, blk), blk), :]

    @pl.when(qi == 0)
    def _head_init():
        rel = (lax.broadcasted_iota(jnp.int32, (blk, blk), 0)
               - lax.broadcasted_iota(jnp.int32, (blk, blk), 1)).astype(F32)
        bias_s[0] = slope * rel
        bias_s[1] = -slope * jnp.abs(rel)
        bias_s[2] = -slope * rel

        lane = lax.broadcasted_iota(jnp.int32, (blk, HEAD_LANES), 1)
        key_offset = lax.broadcasted_iota(jnp.int32, (blk, HEAD_LANES), 0).astype(F32)
        for side, sign in ((0, 1.0), (1, -1.0)):
            a0, a1, a2 = _split_bf16x3((sign * slope) * key_offset)
            feat_s[side] = jnp.where(lane == 0, a0, jnp.where(lane == 1, a1, jnp.where(lane == 2, a2,
                                     jnp.where(lane < 6, 1.0, 0.0)))).astype(BF16)

        first_map = lax.broadcasted_iota(jnp.int32, (blk, HEAD_LANES), 1) < half

        def norm_block(j, best):
            kb = keys(j).astype(F32)
            sq = kb * kb
            n0 = jnp.sqrt(jnp.sum(jnp.where(first_map, sq, 0.0), axis=1, keepdims=True))
            n1 = jnp.sqrt(jnp.sum(jnp.where(first_map, 0.0, sq), axis=1, keepdims=True))
            return jnp.maximum(best[0], jnp.max(n0)), jnp.maximum(best[1], jnp.max(n1))

        kmax = lax.fori_loop(0, n_kblk, norm_block, (jnp.float32(0.0), jnp.float32(0.0)))
        kmax_s[0] = kmax[0]
        kmax_s[1] = kmax[1]

    zeros = jnp.zeros((half, blk), BF16)
    for qb in range(DIFF_QBLOCKS):
        q_s[2 * qb, 0:half, :] = qt_ref[0, 0, qb, 0:half, :]
        q_s[2 * qb, half:, :] = zeros
        q_s[2 * qb + 1, 0:half, :] = zeros
        q_s[2 * qb + 1, half:, :] = qt_ref[0, 0, qb, half:, :]
    m_s[...] = jnp.full(m_s.shape, -jnp.inf, F32)
    l_s[...] = jnp.zeros(l_s.shape, F32)
    acc_s[...] = jnp.zeros(acc_s.shape, F32)

    u_max = jnp.float32(0.0)
    for ci in range(n_chains):
        u = _column_norms(q_s[ci]) * (kmax_s[ci % 2] * NORM_MARGIN)
        u_s[ci] = u
        u_max = jnp.maximum(u_max, jnp.max(u))
    bounded = u_max <= BOUNDED_STABILISER_MAX

    def score(kblk, j, ci):
        return jnp.dot(kblk, q_s[ci], preferred_element_type=F32)

    slope_row = jnp.full((1, blk), slopes_ref[h], F32) * LOG2E

    def origin_gap(j, qblk):
        return jnp.full((1, blk), jnp.abs(j - qblk) * blk, jnp.int32).astype(F32)

    @pl.when(bounded)
    def _bounded_sweep():
        w = windows_ref[h]
        lo = jnp.maximum(q0 - w, 0)
        hi = jnp.minimum(q0 + 1 + w, n_kblk)

        def block_of(t):
            j = lo + t
            return jnp.minimum(j + (j >= q0).astype(jnp.int32), n_kblk - 1)

        query_offset = lax.broadcasted_iota(jnp.int32, (1, blk), 1).astype(F32)
        feature_row = lax.broadcasted_iota(jnp.int32, (FEATURE_ROWS, blk), 0)
        rhs_pad = jnp.zeros((HEAD_LANES - FEATURE_ROWS, blk), BF16)

        def feature_score(kblk, j, ci):
            right = j > q0
            signed_slope = jnp.where(right, -slope_row, slope_row)
            row = -signed_slope * query_offset - slope_row * origin_gap(j, q0) - u_s[ci]
            r0, r1, r2 = _split_bf16x3(row)
            feats = jnp.where(feature_row < 3, 1.0,
                              jnp.where(feature_row == 3, r0,
                                        jnp.where(feature_row == 4, r1,
                                                  jnp.where(feature_row == 5, r2, 0.0))))
            rhs = jnp.concatenate([q_s[ci], feats.astype(BF16), rhs_pad], axis=0)
            lhs = jnp.concatenate([kblk, feat_s[right.astype(jnp.int32)]], axis=1)
            return jnp.dot(lhs, rhs, preferred_element_type=F32)

        def own_block():
            kd = keys(q0)
            vt = vt_ref[0, 0, q0]
            for ci in range(n_chains):
                s = jnp.dot(kd, q_s[ci], preferred_element_type=F32)
                _bounded_softmax_step(s + bias_s[1] - u_s[ci], vt, l_s, acc_s, ci)

        def step(s, ci, j, vt):
            _bounded_softmax_step(s, vt, l_s, acc_s, ci)

        _sweep_key_blocks(0, hi - lo - 1, keys, lambda j: vt_ref[0, 0, j], feature_score, step, n_chains, s_s,
                          unroll=2, block_of=block_of, after_prime=own_block)

    @pl.when(jnp.logical_not(bounded))
    def _online_sweep():
        def step(s, ci, j, vt):
            side = jnp.where(j < q0, 0, jnp.where(j == q0, 1, 2))
            _online_softmax_step(s + bias_s[side], -slope_row * origin_gap(j, q0), vt, m_s, l_s, acc_s, ci)

        _sweep_key_blocks(0, n_kblk, keys, lambda j: vt_ref[0, 0, j], score, step, n_chains, s_s)

    dl = lam_ref[...]
    lam = (jnp.exp(jnp.sum(dl[0:1] * dl[1:2], axis=-1, keepdims=True))
           - jnp.exp(jnp.sum(dl[2:3] * dl[3:4], axis=-1, keepdims=True)) + lambda_init)
    for qb in range(DIFF_QBLOCKS):
        c0, c1 = 2 * qb, 2 * qb + 1
        o = acc_s[c0] * (1.0 / l_s[c0]) - lam * (acc_s[c1] * (1.0 / l_s[c1]))
        ms = jnp.mean(o * o, axis=0, keepdims=True)
        y = o * lax.rsqrt(ms + EPS) * g_ref[...] * (1.0 - lambda_init)
        o_ref[0, qb * blk:(qb + 1) * blk, :] = y.T.astype(o_ref.dtype)


def _alibi_windows(slopes, blk):
    dist = EXP2_UNDERFLOW / (slopes * LOG2E)
    return (jnp.floor((dist - 1.0) / blk) + 1.0).astype(jnp.int32)


def _diff_attention(qt, k, vt, slopes, diff_lambda, subln_g, lambda_init):
    b, nh, nb, _, blk = qt.shape
    s = nb * blk
    body = functools.partial(_diff_attn_kernel, lambda_init=lambda_init, n_kblk=nb)
    qbs = DIFF_QBLOCKS
    assert qbs == 1, "the bounded sweep folds the bias of one query block into the score matmul"
    n_chains = 2 * qbs
    return pl.pallas_call(
        body,
        grid=(b, nh, nb // qbs),
        in_specs=[pl.BlockSpec(memory_space=pltpu.SMEM),
                  pl.BlockSpec(memory_space=pltpu.SMEM),
                  pl.BlockSpec((4, DIFF_HD), lambda bb, hh, qi: (0, 0)),
                  pl.BlockSpec((DIFF_V_HD, 1), lambda bb, hh, qi: (0, 0)),
                  pl.BlockSpec((1, 1, qbs, HEAD_LANES, blk), lambda bb, hh, qi: (bb, hh, qi, 0, 0)),
                  pl.BlockSpec((1, 1, s, HEAD_LANES), lambda bb, hh, qi: (bb, hh, 0, 0)),
                  pl.BlockSpec((1, 1, nb, HEAD_LANES, blk), lambda bb, hh, qi: (bb, hh, 0, 0, 0))],
        out_specs=pl.BlockSpec((1, qbs * blk, HEAD_LANES), lambda bb, hh, qi: (bb, qi, hh)),
        out_shape=jax.ShapeDtypeStruct((b, s, nh * HEAD_LANES), BF16),
        scratch_shapes=[pltpu.VMEM((n_chains, HEAD_LANES, blk), BF16),
                        pltpu.VMEM((3, blk, blk), F32),
                        pltpu.VMEM((2, blk, HEAD_LANES), BF16),
                        pltpu.VMEM((DIFF_LOOKAHEAD, blk, blk), F32),
                        pltpu.VMEM((n_chains, 1, blk), F32),
                        pltpu.VMEM((n_chains, 1, blk), F32),
                        pltpu.VMEM((n_chains, 1, blk), F32),
                        pltpu.VMEM((n_chains, HEAD_LANES, blk), F32),
                        pltpu.SMEM((2,), F32)],
        compiler_params=pltpu.CompilerParams(dimension_semantics=("arbitrary",) * 3,
                                             vmem_limit_bytes=VMEM_LIMIT_BYTES),
        name="diff_attention",
    )(slopes, _alibi_windows(slopes, blk), diff_lambda, subln_g.reshape(DIFF_V_HD, 1), qt, k, vt)


def _gqa_attn_kernel(qt_ref, k_ref, vt_ref, o_ref, s_s, u_s, m_s, l_s, acc_s, kmax_s, *, n_kblk):
    blk = ATT_BLK
    qi = pl.program_id(2)

    def keys(j):
        return k_ref[0, 0, pl.ds(pl.multiple_of(j * blk, blk), blk), :]

    @pl.when(qi == 0)
    def _kv_head_init():
        def norm_block(j, best):
            kb = keys(j).astype(F32)
            return jnp.maximum(best, jnp.max(jnp.sqrt(jnp.sum(kb * kb, axis=1, keepdims=True))))

        kmax_s[0] = lax.fori_loop(0, n_kblk, norm_block, jnp.float32(0.0))

    m_s[...] = jnp.full(m_s.shape, -jnp.inf, F32)
    l_s[...] = jnp.zeros(l_s.shape, F32)
    acc_s[...] = jnp.zeros(acc_s.shape, F32)

    u_max = jnp.float32(0.0)
    for g in range(GQA_GROUP):
        u = _column_norms(qt_ref[0, g, 0]) * (kmax_s[0] * NORM_MARGIN)
        u_s[g] = u
        u_max = jnp.maximum(u_max, jnp.max(u))
    bounded = u_max <= BOUNDED_STABILISER_MAX

    def score(kblk, j, g):
        return jnp.dot(kblk, qt_ref[0, g, 0], preferred_element_type=F32)

    @pl.when(bounded)
    def _bounded_sweep():
        def step(s, g, j, vt):
            _bounded_softmax_step(s - u_s[g], vt, l_s, acc_s, g)

        _sweep_key_blocks(0, n_kblk, keys, lambda j: vt_ref[0, 0, j], score, step, GQA_GROUP, s_s, unroll=4)

    @pl.when(jnp.logical_not(bounded))
    def _online_sweep():
        def step(s, g, j, vt):
            _online_softmax_step(s, 0.0, vt, m_s, l_s, acc_s, g)

        _sweep_key_blocks(0, n_kblk, keys, lambda j: vt_ref[0, 0, j], score, step, GQA_GROUP, s_s)

    for g in range(GQA_GROUP):
        o = acc_s[g] * (1.0 / l_s[g])
        o_ref[0, :, g * HEAD_LANES:(g + 1) * HEAD_LANES] = o.T.astype(o_ref.dtype)


def _gqa_attention(qt, k, vt):
    b, nh, nb, _, blk = qt.shape
    s = nb * blk
    nkv = nh // GQA_GROUP
    body = functools.partial(_gqa_attn_kernel, n_kblk=nb)
    return pl.pallas_call(
        body,
        grid=(b, nkv, nb),
        in_specs=[pl.BlockSpec((1, GQA_GROUP, 1, HEAD_LANES, blk), lambda bb, kv, qi: (bb, kv, qi, 0, 0)),
                  pl.BlockSpec((1, 1, s, HEAD_LANES), lambda bb, kv, qi: (bb, kv, 0, 0)),
                  pl.BlockSpec((1, 1, nb, HEAD_LANES, blk), lambda bb, kv, qi: (bb, kv, 0, 0, 0))],
        out_specs=pl.BlockSpec((1, blk, GQA_GROUP * HEAD_LANES), lambda bb, kv, qi: (bb, qi, kv)),
        out_shape=jax.ShapeDtypeStruct((b, s, nh * HEAD_LANES), BF16),
        scratch_shapes=[pltpu.VMEM((GQA_LOOKAHEAD, blk, blk), F32),
                        pltpu.VMEM((GQA_GROUP, 1, blk), F32),
                        pltpu.VMEM((GQA_GROUP, 1, blk), F32),
                        pltpu.VMEM((GQA_GROUP, 1, blk), F32),
                        pltpu.VMEM((GQA_GROUP, HEAD_LANES, blk), F32),
                        pltpu.SMEM((1,), F32)],
        compiler_params=pltpu.CompilerParams(dimension_semantics=("arbitrary",) * 3,
                                             vmem_limit_bytes=VMEM_LIMIT_BYTES),
        name="gqa_attention",
    )(qt, k, vt)


def _merge_kernel(oa_ref, ob_ref, wa_ref, wb_ref, ga_ref, gb_ref, o_ref):
    a = jnp.dot(oa_ref[...], wa_ref[...], preferred_element_type=F32)
    bb = jnp.dot(ob_ref[...], wb_ref[...], preferred_element_type=F32)
    o_ref[...] = (ga_ref[...] * a + gb_ref[...] * bb).astype(o_ref.dtype)


def _merge(oa, ob, wa, wb, gate, tm=1024, tn=1024):
    t = oa.shape[0]
    gb0 = D_MODEL // tn
    return pl.pallas_call(
        _merge_kernel,
        grid=(t // tm, D_MODEL // tn),
        in_specs=[pl.BlockSpec((tm, DIFF_WIDTH), lambda i, j: (i, 0)),
                  pl.BlockSpec((tm, GQA_WIDTH), lambda i, j: (i, 0)),
                  pl.BlockSpec((DIFF_WIDTH, tn), lambda i, j: (0, j)),
                  pl.BlockSpec((GQA_WIDTH, tn), lambda i, j: (0, j)),
                  pl.BlockSpec((tm, tn), lambda i, j: (i, j)),
                  pl.BlockSpec((tm, tn), lambda i, j: (i, gb0 + j))],
        out_specs=pl.BlockSpec((tm, tn), lambda i, j: (i, j)),
        out_shape=jax.ShapeDtypeStruct((t, D_MODEL), BF16),
        compiler_params=_params(2),
        name="merge",
    )(oa, ob, wa, wb, gate, gate)


def _out_proj_kernel(a_ref, w_ref, x_ref, g_ref, gn_ref, o_ref, h_ref):
    half = a_ref.shape[0] // 2
    for r in range(2):
        rows = slice(r * half, (r + 1) * half)
        m = jnp.dot(a_ref[rows, :], w_ref[...], preferred_element_type=F32)
        x = x_ref[rows, :] + _rms(m, g_ref[...])
        o_ref[rows, :] = x
        h_ref[rows, :] = _rms(x, gn_ref[...]).astype(h_ref.dtype)


def _out_proj(a, w, x2d, g, g_next, tm=512):
    t = a.shape[0]
    return pl.pallas_call(
        _out_proj_kernel,
        grid=(t // tm,),
        in_specs=[pl.BlockSpec((tm, D_MODEL), lambda i: (i, 0)),
                  pl.BlockSpec((D_MODEL, D_MODEL), lambda i: (0, 0)),
                  pl.BlockSpec((tm, D_MODEL), lambda i: (i, 0)),
                  pl.BlockSpec((1, D_MODEL), lambda i: (0, 0)),
                  pl.BlockSpec((1, D_MODEL), lambda i: (0, 0))],
        out_specs=[pl.BlockSpec((tm, D_MODEL), lambda i: (i, 0)),
                   pl.BlockSpec((tm, D_MODEL), lambda i: (i, 0))],
        out_shape=[jax.ShapeDtypeStruct((t, D_MODEL), F32),
                   jax.ShapeDtypeStruct((t, D_MODEL), BF16)],
        compiler_params=_params(1),
        name="out_proj",
    )(a, w, x2d, g.reshape(1, D_MODEL), g_next.reshape(1, D_MODEL))


def _swiglu_kernel(h_ref, wg_ref, wu_ref, o_ref):
    h = h_ref[...]
    gte = jnp.dot(h, wg_ref[...], preferred_element_type=F32)
    up = jnp.dot(h, wu_ref[...], preferred_element_type=F32)
    o_ref[...] = (jax.nn.silu(gte) * up).astype(o_ref.dtype)


def _swiglu(h2d, w_gate_up, tm=1024, tn=512):
    t = h2d.shape[0]
    ub0 = D_FF // tn
    return pl.pallas_call(
        _swiglu_kernel,
        grid=(t // tm, D_FF // tn),
        in_specs=[pl.BlockSpec((tm, D_MODEL), lambda i, j: (i, 0)),
                  pl.BlockSpec((D_MODEL, tn), lambda i, j: (0, j)),
                  pl.BlockSpec((D_MODEL, tn), lambda i, j: (0, ub0 + j))],
        out_specs=pl.BlockSpec((tm, tn), lambda i, j: (i, j)),
        out_shape=jax.ShapeDtypeStruct((t, D_FF), BF16),
        compiler_params=_params(2),
        name="swiglu",
    )(h2d, w_gate_up, w_gate_up)


def _down_kernel(a_ref, w_ref, x_ref, g_ref, *rest, with_next):
    if with_next:
        gn_ref, o_ref, h_ref = rest
    else:
        (o_ref,) = rest
    m = jnp.dot(a_ref[...], w_ref[...], preferred_element_type=F32)
    x = x_ref[...] + _rms(m, g_ref[...])
    o_ref[...] = x
    if with_next:
        h_ref[...] = _rms(x, gn_ref[...]).astype(h_ref.dtype)


def _down_proj(a, w, x2d, g, g_next=None, tm=256):
    t = a.shape[0]
    with_next = g_next is not None
    row_spec = pl.BlockSpec((tm, D_MODEL), lambda i: (i, 0))
    gain_spec = pl.BlockSpec((1, D_MODEL), lambda i: (0, 0))
    in_specs = [pl.BlockSpec((tm, D_FF), lambda i: (i, 0)),
                pl.BlockSpec((D_FF, D_MODEL), lambda i: (0, 0), pipeline_mode=pl.Buffered(1)),
                row_spec, gain_spec]
    args = [a, w, x2d, g.reshape(1, D_MODEL)]
    out_specs = [row_spec]
    out_shape = [jax.ShapeDtypeStruct((t, D_MODEL), F32)]
    if with_next:
        in_specs.append(gain_spec)
        args.append(g_next.reshape(1, D_MODEL))
        out_specs.append(row_spec)
        out_shape.append(jax.ShapeDtypeStruct((t, D_MODEL), BF16))
    out = pl.pallas_call(
        functools.partial(_down_kernel, with_next=with_next),
        grid=(t // tm,),
        in_specs=in_specs, out_specs=out_specs, out_shape=out_shape,
        compiler_params=_params(1),
        name="down_proj",
    )(*args)
    return (out[0], out[1]) if with_next else (out[0], None)


def _rope_tables(s):
    rows = s // GRID_W
    row = jnp.repeat(jnp.arange(rows, dtype=F32), GRID_W)
    col = jnp.tile(jnp.arange(GRID_W, dtype=F32), rows)
    quarter = GQA_HD // 4
    freqs = ROPE_THETA ** (-jnp.arange(quarter, dtype=F32) / quarter)
    ang_r = row[:, None] * freqs[None, :]
    ang_c = col[:, None] * freqs[None, :]
    cos = jnp.concatenate([jnp.cos(ang_r), jnp.cos(ang_c), jnp.cos(ang_r), jnp.cos(ang_c)], axis=-1)
    sin = jnp.concatenate([-jnp.sin(ang_r), -jnp.sin(ang_c), jnp.sin(ang_r), jnp.sin(ang_c)], axis=-1)
    return cos, sin


def _reorder_head_lanes(a):
    heads = a.reshape(a.shape[:-1] + (a.shape[-1] // GQA_HD, GQA_HD))
    return heads[..., jnp.asarray(_rope_lane_order())].reshape(a.shape)


def _trunk(x, weights):
    b, s, _ = x.shape
    x2d = x.reshape(b * s, D_MODEL)
    cos, sin = _rope_tables(s)
    slopes = jnp.exp2(-8.0 * jnp.arange(1, DIFF_HEADS + 1, dtype=F32) / DIFF_HEADS)
    h = _prenorm(x2d, weights[0][0])
    for l in range(DEPTH):
        (mix_pre_g, w_in, diff_lambda, diff_subln_g, gqa_qnorm_g, gqa_knorm_g, w_proj_a, w_proj_b,
         w_out, mix_post_g, ffn_pre_g, w_gate_up, w_down, ffn_post_g) = [w[l] for w in weights]
        lambda_init = 0.8 - 0.6 * math.exp(-0.3 * l)
        dqt = _proj_heads(h, w_in, b, s, OFF_DQ, DIFF_QK_WIDTH, transpose=True,
                          scale=DIFF_HD ** -0.5 * LOG2E)
        dk = _proj_heads(h, w_in, b, s, OFF_DK, DIFF_QK_WIDTH, transpose=False)
        dvt = _proj_heads(h, w_in, b, s, OFF_DV, DIFF_WIDTH, transpose=True)
        o_a = _diff_attention(dqt, dk, dvt, slopes, diff_lambda, diff_subln_g, lambda_init)
        w_rope = _reorder_head_lanes(w_in[:, OFF_GQ:OFF_GV])
        gqt = _proj_heads(h, w_rope, b, s, 0, GQA_WIDTH, transpose=True, scale=GQA_HD ** -0.5 * LOG2E,
                          rope=(_reorder_head_lanes(gqa_qnorm_g), cos, sin))
        gk = _proj_heads(h, w_rope, b, s, GQA_WIDTH, GQA_KV_WIDTH, transpose=False,
                         rope=(_reorder_head_lanes(gqa_knorm_g), cos, sin))
        gvt = _proj_heads(h, w_in, b, s, OFF_GV, GQA_KV_WIDTH, transpose=True)
        o_b = _gqa_attention(gqt, gk, gvt)
        gate = _proj_gate(h, w_in[:, OFF_GATE:])
        merged = _merge(o_a.reshape(b * s, DIFF_WIDTH), o_b.reshape(b * s, GQA_WIDTH),
                        w_proj_a, w_proj_b, gate)
        x2d, h = _out_proj(merged, w_out, x2d, mix_post_g, ffn_pre_g)
        act = _swiglu(h, w_gate_up)
        next_pre_g = weights[0][l + 1] if l + 1 < DEPTH else None
        x2d, h = _down_proj(act, w_down, x2d, ffn_post_g, next_pre_g)
    return x2d.reshape(b, s, D_MODEL)


def kernel(x_prompt, x_sample, mix_pre_g, w_in, diff_lambda, diff_subln_g, gqa_qnorm_g, gqa_knorm_g,
           w_proj_a, w_proj_b, w_out, mix_post_g, ffn_pre_g, w_gate_up, w_down, ffn_post_g):
    weights = (mix_pre_g, w_in.astype(BF16), diff_lambda, diff_subln_g, gqa_qnorm_g, gqa_knorm_g,
               w_proj_a.astype(BF16), w_proj_b.astype(BF16), w_out.astype(BF16), mix_post_g,
               ffn_pre_g, w_gate_up.astype(BF16), w_down.astype(BF16), ffn_post_g)
    return (_trunk(x_prompt, weights), _trunk(x_sample, weights))
```

```python
import functools
import math

import jax
import jax.numpy as jnp
from jax import lax
from jax.experimental import pallas as pl
from jax.experimental.pallas import tpu as pltpu

D_MODEL = 2048
DEPTH = 2
GRID_W = 64
EPS = 1e-6
DIFF_HEADS = 8
DIFF_HD = 64
DIFF_V_HD = 128
DIFF_QK_WIDTH = 1024
DIFF_WIDTH = 1024
GQA_HEADS = 8
GQA_KV_HEADS = 2
GQA_GROUP = GQA_HEADS // GQA_KV_HEADS
GQA_HD = 128
GQA_WIDTH = 1024
GQA_KV_WIDTH = 256
ROPE_THETA = 10000.0
GATE_WIDTH = 2 * D_MODEL
D_FF = 5632

HEAD_LANES = 128
OFF_DQ = 0
OFF_DK = OFF_DQ + DIFF_QK_WIDTH
OFF_DV = OFF_DK + DIFF_QK_WIDTH
OFF_GQ = OFF_DV + DIFF_WIDTH
OFF_GK = OFF_GQ + GQA_WIDTH
OFF_GV = OFF_GK + GQA_KV_WIDTH
OFF_GATE = OFF_GV + GQA_KV_WIDTH

VMEM_LIMIT_BYTES = 52 * 1024 * 1024
ATT_BLK = 512
DIFF_LOOKAHEAD = 1
DIFF_QBLOCKS = 1
GQA_LOOKAHEAD = 1
LOG2E = math.log2(math.e)

BF16 = jnp.bfloat16
F32 = jnp.float32


def _params(n_grid):
    return pltpu.CompilerParams(
        dimension_semantics=("parallel",) * (n_grid - 1) + ("arbitrary",),
        vmem_limit_bytes=VMEM_LIMIT_BYTES)


def _rms(x, g):
    ms = jnp.mean(x * x, axis=-1, keepdims=True)
    return x * lax.rsqrt(ms + EPS) * g


def _prenorm_kernel(x_ref, g_ref, o_ref):
    o_ref[...] = _rms(x_ref[...], g_ref[...]).astype(o_ref.dtype)


def _prenorm(x2d, g, tm=1024):
    t = x2d.shape[0]
    return pl.pallas_call(
        _prenorm_kernel,
        grid=(t // tm,),
        in_specs=[pl.BlockSpec((tm, D_MODEL), lambda i: (i, 0)),
                  pl.BlockSpec((1, D_MODEL), lambda i: (0, 0))],
        out_specs=pl.BlockSpec((tm, D_MODEL), lambda i: (i, 0)),
        out_shape=jax.ShapeDtypeStruct((t, D_MODEL), BF16),
        compiler_params=_params(1),
        name="prenorm",
    )(x2d, g.reshape(1, D_MODEL))


def _rope_lane_order():
    quarter = GQA_HD // 4
    return [i + off for off in (0, 2 * quarter, quarter, 3 * quarter) for i in range(quarter)]


def _rope(y, cos, sin_signed):
    return y * cos + pltpu.roll(y, GQA_HD // 2, 1) * sin_signed


def _head_mean_square(y):
    sq = (y * y).astype(BF16)
    ones = jnp.ones((HEAD_LANES, HEAD_LANES), BF16)
    return jnp.dot(sq, ones, preferred_element_type=F32) * (1.0 / HEAD_LANES)


def _store_head(o_ref, hh, y, transpose):
    if transpose:
        for c in range(y.shape[0] // ATT_BLK):
            o_ref[0, hh, c] = y[c * ATT_BLK:(c + 1) * ATT_BLK].T.astype(o_ref.dtype)
    else:
        o_ref[0, hh] = y.astype(o_ref.dtype)


def _seg_heads_kernel(h_ref, w_ref, o_ref, *, scale, transpose, n_heads):
    acc = jnp.dot(h_ref[...], w_ref[...], preferred_element_type=F32)
    for hh in range(n_heads):
        y = acc[:, hh * HEAD_LANES:(hh + 1) * HEAD_LANES]
        if scale != 1.0:
            y = y * scale
        _store_head(o_ref, hh, y, transpose)


def _seg_rope_kernel(h_ref, w_ref, g_ref, cos_ref, sin_ref, o_ref, *, scale, transpose, n_heads):
    acc = jnp.dot(h_ref[...], w_ref[...], preferred_element_type=F32)
    cos = cos_ref[...]
    sin = sin_ref[...]
    g = g_ref[...]
    for hh in range(n_heads):
        y = acc[:, hh * HEAD_LANES:(hh + 1) * HEAD_LANES]
        y = y * lax.rsqrt(_head_mean_square(y) + EPS) * g
        y = _rope(y, cos, sin)
        if scale != 1.0:
            y = y * scale
        _store_head(o_ref, hh, y, transpose)


def _seg_gate_kernel(h_ref, w_ref, o_ref):
    acc = jnp.dot(h_ref[...], w_ref[...], preferred_element_type=F32)
    o_ref[...] = jax.nn.sigmoid(acc).astype(o_ref.dtype)


def _proj_heads(h2d, w_in, b, s, col0, ncols, *, transpose, scale=1.0, rope=None, tm=1024, tn=1024):
    tn = min(tn, ncols)
    n_heads_blk = tn // HEAD_LANES
    n_heads = ncols // HEAD_LANES
    nsb = s // tm
    grid = (b * nsb, ncols // tn)
    cb0 = col0 // tn
    assert col0 % tn == 0 and s % tm == 0
    in_specs = [pl.BlockSpec((tm, D_MODEL), lambda i, j: (i, 0)),
                pl.BlockSpec((D_MODEL, tn), lambda i, j: (0, cb0 + j))]
    args = [h2d, w_in]
    if rope is None:
        body = functools.partial(_seg_heads_kernel, scale=scale, transpose=transpose, n_heads=n_heads_blk)
    else:
        gain, cos, sin = rope
        in_specs += [pl.BlockSpec((1, HEAD_LANES), lambda i, j: (0, 0)),
                     pl.BlockSpec((tm, HEAD_LANES), lambda i, j: (i % nsb, 0)),
                     pl.BlockSpec((tm, HEAD_LANES), lambda i, j: (i % nsb, 0))]
        args += [gain.reshape(1, HEAD_LANES), cos, sin]
        body = functools.partial(_seg_rope_kernel, scale=scale, transpose=transpose, n_heads=n_heads_blk)
    if transpose:
        out_shape = (b, n_heads, s // ATT_BLK, HEAD_LANES, ATT_BLK)
        out_spec = pl.BlockSpec((1, n_heads_blk, tm // ATT_BLK, HEAD_LANES, ATT_BLK),
                                lambda i, j: (i // nsb, j, i % nsb, 0, 0))
    else:
        out_shape = (b, n_heads, s, HEAD_LANES)
        out_spec = pl.BlockSpec((1, n_heads_blk, tm, HEAD_LANES), lambda i, j: (i // nsb, j, i % nsb, 0))
    return pl.pallas_call(
        body, grid=grid, in_specs=in_specs, out_specs=out_spec,
        out_shape=jax.ShapeDtypeStruct(out_shape, BF16),
        compiler_params=_params(2),
        name="proj_heads",
    )(*args)


def _proj_gate(h2d, w_gate, tm=1024, tn=1024):
    t = h2d.shape[0]
    return pl.pallas_call(
        _seg_gate_kernel,
        grid=(t // tm, GATE_WIDTH // tn),
        in_specs=[pl.BlockSpec((tm, D_MODEL), lambda i, j: (i, 0)),
                  pl.BlockSpec((D_MODEL, tn), lambda i, j: (0, j))],
        out_specs=pl.BlockSpec((tm, tn), lambda i, j: (i, j)),
        out_shape=jax.ShapeDtypeStruct((t, GATE_WIDTH), F32),
        compiler_params=_params(2),
        name="proj_gate",
    )(h2d, w_gate)


BOUNDED_STABILISER_MAX = 32.0
NORM_MARGIN = 1.02
EXP2_UNDERFLOW = 127.0
FEATURE_ROWS = 16


def _online_softmax_step(s, c, vt, m_ref, l_ref, acc_ref, idx):
    m_old = m_ref[idx]
    m_new = jnp.maximum(m_old, jnp.max(s, axis=0, keepdims=True) + c)
    p = jnp.exp2(s - (m_new - c))
    alpha = jnp.exp2(m_old - m_new)
    l_ref[idx] = alpha * l_ref[idx] + jnp.sum(p, axis=0, keepdims=True)
    acc_ref[idx] = alpha * acc_ref[idx] + jnp.dot(vt, p.astype(BF16), preferred_element_type=F32)
    m_ref[idx] = m_new


def _bounded_softmax_step(t, vt, l_ref, acc_ref, idx):
    p = jnp.exp2(t)
    l_ref[idx] = l_ref[idx] + jnp.sum(p, axis=0, keepdims=True)
    acc_ref[idx] = acc_ref[idx] + jnp.dot(vt, p.astype(BF16), preferred_element_type=F32)


def _sweep_key_blocks(lo, hi, keys, values_t, score, step, n_chains, s_s, unroll=1,
                      block_of=lambda t: t, after_prime=None):
    ahead = s_s.shape[0]
    j_first = block_of(lo)
    k_first = keys(j_first)
    for i in range(ahead):
        s_s[i] = score(k_first, j_first, i)
    if after_prime is not None:
        after_prime()

    def key_blocks(t0, count):
        queue = [s_s[i] for i in range(ahead)]
        for dt in range(count):
            j = block_of(t0 + dt)
            j_next = block_of(jnp.minimum(t0 + dt + 1, hi - 1))
            k = keys(j)
            k_next = keys(j_next)
            vt = values_t(j)
            for ci in range(n_chains):
                nxt = ci + ahead
                queue.append(score(k, j, nxt) if nxt < n_chains else score(k_next, j_next, nxt - n_chains))
                step(queue.pop(0), ci, j, vt)
        for i in range(ahead):
            s_s[i] = queue[i]

    sizes = unroll if isinstance(unroll, tuple) else tuple(dict.fromkeys((unroll, 1)))
    pos = lo
    for size in sizes:
        n_groups = (hi - pos) // size
        if isinstance(n_groups, int) and n_groups == 0:
            continue

        def group(i, carry, pos=pos, size=size):
            key_blocks(pos + i * size, size)
            return carry

        lax.fori_loop(0, n_groups, group, 0)
        pos = pos + n_groups * size


def _split_bf16x3(x):
    p0 = x.astype(BF16).astype(F32)
    p1 = (x - p0).astype(BF16).astype(F32)
    p2 = (x - p0 - p1).astype(BF16).astype(F32)
    return p0, p1, p2


def _column_norms(x_t):
    x = x_t.astype(F32)
    return jnp.sqrt(jnp.sum(x * x, axis=0, keepdims=True))


def _diff_attn_kernel(slopes_ref, windows_ref, lam_ref, g_ref, qt_ref, k_ref, vt_ref, o_ref,
                      q_s, bias_s, feat_s, s_s, u_s, m_s, l_s, acc_s, kmax_s, *, lambda_init, n_kblk):
    blk = ATT_BLK
    n_chains = 2 * DIFF_QBLOCKS
    h = pl.program_id(1)
    qi = pl.program_id(2)
    q0 = qi * DIFF_QBLOCKS
    slope = slopes_ref[h] * LOG2E
    half = DIFF_HD

    def keys(j):
        return k_ref[0, 0, pl.ds(pl.multiple_of(j * blk, blk), blk), :]

    @pl.when(qi == 0)
    def _head_init():
        rel = (lax.broadcasted_iota(jnp.int32, (blk, blk), 0)
               - lax.broadcasted_iota(jnp.int32, (blk, blk), 1)).astype(F32)
        bias_s[0] = slope * rel
        bias_s[1] = -slope * jnp.abs(rel)
        bias_s[2] = -slope * rel

        lane = lax.broadcasted_iota(jnp.int32, (blk, HEAD_LANES), 1)
        key_offset = lax.broadcasted_iota(jnp.int32, (blk, HEAD_LANES), 0).astype(F32)
        for side, sign in ((0, 1.0), (1, -1.0)):
            a0, a1, a2 = _split_bf16x3((sign * slope) * key_offset)
            feat_s[side] = jnp.where(lane == 0, a0, jnp.where(lane == 1, a1, jnp.where(lane == 2, a2,
                                     jnp.where(lane < 6, 1.0, 0.0)))).astype(BF16)

        first_map = lax.broadcasted_iota(jnp.int32, (blk, HEAD_LANES), 1) < half

        def norm_block(j, best):
            kb = keys(j).astype(F32)
            sq = kb * kb
            n0 = jnp.sqrt(jnp.sum(jnp.where(first_map, sq, 0.0), axis=1, keepdims=True))
            n1 = jnp.sqrt(jnp.sum(jnp.where(first_map, 0.0, sq), axis=1, keepdims=True))
            return jnp.maximum(best[0], jnp.max(n0)), jnp.maximum(best[1], jnp.max(n1))

        kmax = lax.fori_loop(0, n_kblk, norm_block, (jnp.float32(0.0), jnp.float32(0.0)))
        kmax_s[0] = kmax[0]
        kmax_s[1] = kmax[1]

    zeros = jnp.zeros((half, blk), BF16)
    for qb in range(DIFF_QBLOCKS):
        q_s[2 * qb, 0:half, :] = qt_ref[0, 0, qb, 0:half, :]
        q_s[2 * qb, half:, :] = zeros
        q_s[2 * qb + 1, 0:half, :] = zeros
        q_s[2 * qb + 1, half:, :] = qt_ref[0, 0, qb, half:, :]
    m_s[...] = jnp.full(m_s.shape, -jnp.inf, F32)
    l_s[...] = jnp.zeros(l_s.shape, F32)
    acc_s[...] = jnp.zeros(acc_s.shape, F32)

    u_max = jnp.float32(0.0)
    for ci in range(n_chains):
        u = _column_norms(q_s[ci]) * (kmax_s[ci % 2] * NORM_MARGIN)
        u_s[ci] = u
        u_max = jnp.maximum(u_max, jnp.max(u))
    bounded = u_max <= BOUNDED_STABILISER_MAX

    def score(kblk, j, ci):
        return jnp.dot(kblk, q_s[ci], preferred_element_type=F32)

    slope_row = jnp.full((1, blk), slopes_ref[h], F32) * LOG2E

    def origin_gap(j, qblk):
        return jnp.full((1, blk), jnp.abs(j - qblk) * blk, jnp.int32).astype(F32)

    @pl.when(bounded)
    def _bounded_sweep():
        w = windows_ref[h]
        lo = jnp.maximum(q0 - w, 0)
        hi = jnp.minimum(q0 + 1 + w, n_kblk)

        def block_of(t):
            j = lo + t
            return jnp.minimum(j + (j >= q0).astype(jnp.int32), n_kblk - 1)

        query_offset = lax.broadcasted_iota(jnp.int32, (1, blk), 1).astype(F32)
        feature_row = lax.broadcasted_iota(jnp.int32, (FEATURE_ROWS, blk), 0)
        rhs_pad = jnp.zeros((HEAD_LANES - FEATURE_ROWS, blk), BF16)

        def feature_score(kblk, j, ci):
            right = j > q0
            signed_slope = jnp.where(right, -slope_row, slope_row)
            row = -signed_slope * query_offset - slope_row * origin_gap(j, q0) - u_s[ci]
            r0, r1, r2 = _split_bf16x3(row)
            feats = jnp.where(feature_row < 3, 1.0,
                              jnp.where(feature_row == 3, r0,
                                        jnp.where(feature_row == 4, r1,
                                                  jnp.where(feature_row == 5, r2, 0.0))))
            rhs = jnp.concatenate([q_s[ci], feats.astype(BF16), rhs_pad], axis=0)
            lhs = jnp.concatenate([kblk, feat_s[right.astype(jnp.int32)]], axis=1)
            return jnp.dot(lhs, rhs, preferred_element_type=F32)

        def own_block():
            kd = keys(q0)
            vt = vt_ref[0, 0, q0]
            for ci in range(n_chains):
                s = jnp.dot(kd, q_s[ci], preferred_element_type=F32)
                _bounded_softmax_step(s + bias_s[1] - u_s[ci], vt, l_s, acc_s, ci)

        def step(s, ci, j, vt):
            _bounded_softmax_step(s, vt, l_s, acc_s, ci)

        _sweep_key_blocks(0, hi - lo - 1, keys, lambda j: vt_ref[0, 0, j], feature_score, step, n_chains, s_s,
                          unroll=(4, 2, 1), block_of=block_of, after_prime=own_block)

    @pl.when(jnp.logical_not(bounded))
    def _online_sweep():
        def step(s, ci, j, vt):
            side = jnp.where(j < q0, 0, jnp.where(j == q0, 1, 2))
            _online_softmax_step(s + bias_s[side], -slope_row * origin_gap(j, q0), vt, m_s, l_s, acc_s, ci)

        _sweep_key_blocks(0, n_kblk, keys, lambda j: vt_ref[0, 0, j], score, step, n_chains, s_s)

    dl = lam_ref[...]
    lam = (jnp.exp(jnp.sum(dl[0:1] * dl[1:2], axis=-1, keepdims=True))
           - jnp.exp(jnp.sum(dl[2:3] * dl[3:4], axis=-1, keepdims=True)) + lambda_init)
    for qb in range(DIFF_QBLOCKS):
        c0, c1 = 2 * qb, 2 * qb + 1
        o = acc_s[c0] * (1.0 / l_s[c0]) - lam * (acc_s[c1] * (1.0 / l_s[c1]))
        ms = jnp.mean(o * o, axis=0, keepdims=True)
        y = o * lax.rsqrt(ms + EPS) * g_ref[...] * (1.0 - lambda_init)
        o_ref[0, qb * blk:(qb + 1) * blk, :] = y.T.astype(o_ref.dtype)


def _alibi_windows(slopes, blk):
    dist = EXP2_UNDERFLOW / (slopes * LOG2E)
    return (jnp.floor((dist - 1.0) / blk) + 1.0).astype(jnp.int32)


def _diff_attention(qt, k, vt, slopes, diff_lambda, subln_g, lambda_init):
    b, nh, nb, _, blk = qt.shape
    s = nb * blk
    body = functools.partial(_diff_attn_kernel, lambda_init=lambda_init, n_kblk=nb)
    qbs = DIFF_QBLOCKS
    assert qbs == 1, "the bounded sweep folds the bias of one query block into the score matmul"
    n_chains = 2 * qbs
    return pl.pallas_call(
        body,
        grid=(b, nh, nb // qbs),
        in_specs=[pl.BlockSpec(memory_space=pltpu.SMEM),
                  pl.BlockSpec(memory_space=pltpu.SMEM),
                  pl.BlockSpec((4, DIFF_HD), lambda bb, hh, qi: (0, 0)),
                  pl.BlockSpec((DIFF_V_HD, 1), lambda bb, hh, qi: (0, 0)),
                  pl.BlockSpec((1, 1, qbs, HEAD_LANES, blk), lambda bb, hh, qi: (bb, hh, qi, 0, 0)),
                  pl.BlockSpec((1, 1, s, HEAD_LANES), lambda bb, hh, qi: (bb, hh, 0, 0)),
                  pl.BlockSpec((1, 1, nb, HEAD_LANES, blk), lambda bb, hh, qi: (bb, hh, 0, 0, 0))],
        out_specs=pl.BlockSpec((1, qbs * blk, HEAD_LANES), lambda bb, hh, qi: (bb, qi, hh)),
        out_shape=jax.ShapeDtypeStruct((b, s, nh * HEAD_LANES), BF16),
        scratch_shapes=[pltpu.VMEM((n_chains, HEAD_LANES, blk), BF16),
                        pltpu.VMEM((3, blk, blk), F32),
                        pltpu.VMEM((2, blk, HEAD_LANES), BF16),
                        pltpu.VMEM((DIFF_LOOKAHEAD, blk, blk), F32),
                        pltpu.VMEM((n_chains, 1, blk), F32),
                        pltpu.VMEM((n_chains, 1, blk), F32),
                        pltpu.VMEM((n_chains, 1, blk), F32),
                        pltpu.VMEM((n_chains, HEAD_LANES, blk), F32),
                        pltpu.SMEM((2,), F32)],
        compiler_params=pltpu.CompilerParams(dimension_semantics=("arbitrary",) * 3,
                                             vmem_limit_bytes=VMEM_LIMIT_BYTES),
        name="diff_attention",
    )(slopes, _alibi_windows(slopes, blk), diff_lambda, subln_g.reshape(DIFF_V_HD, 1), qt, k, vt)


def _gqa_attn_kernel(qt_ref, k_ref, vt_ref, o_ref, s_s, u_s, m_s, l_s, acc_s, kmax_s, *, n_kblk):
    blk = ATT_BLK
    qi = pl.program_id(2)

    def keys(j):
        return k_ref[0, 0, pl.ds(pl.multiple_of(j * blk, blk), blk), :]

    @pl.when(qi == 0)
    def _kv_head_init():
        def norm_block(j, best):
            kb = keys(j).astype(F32)
            return jnp.maximum(best, jnp.max(jnp.sqrt(jnp.sum(kb * kb, axis=1, keepdims=True))))

        kmax_s[0] = lax.fori_loop(0, n_kblk, norm_block, jnp.float32(0.0))

    m_s[...] = jnp.full(m_s.shape, -jnp.inf, F32)
    l_s[...] = jnp.zeros(l_s.shape, F32)
    acc_s[...] = jnp.zeros(acc_s.shape, F32)

    u_max = jnp.float32(0.0)
    for g in range(GQA_GROUP):
        u = _column_norms(qt_ref[0, g, 0]) * (kmax_s[0] * NORM_MARGIN)
        u_s[g] = u
        u_max = jnp.maximum(u_max, jnp.max(u))
    bounded = u_max <= BOUNDED_STABILISER_MAX

    def score(kblk, j, g):
        return jnp.dot(kblk, qt_ref[0, g, 0], preferred_element_type=F32)

    @pl.when(bounded)
    def _bounded_sweep():
        def step(s, g, j, vt):
            _bounded_softmax_step(s - u_s[g], vt, l_s, acc_s, g)

        _sweep_key_blocks(0, n_kblk, keys, lambda j: vt_ref[0, 0, j], score, step, GQA_GROUP, s_s, unroll=4)

    @pl.when(jnp.logical_not(bounded))
    def _online_sweep():
        def step(s, g, j, vt):
            _online_softmax_step(s, 0.0, vt, m_s, l_s, acc_s, g)

        _sweep_key_blocks(0, n_kblk, keys, lambda j: vt_ref[0, 0, j], score, step, GQA_GROUP, s_s)

    for g in range(GQA_GROUP):
        o = acc_s[g] * (1.0 / l_s[g])
        o_ref[0, :, g * HEAD_LANES:(g + 1) * HEAD_LANES] = o.T.astype(o_ref.dtype)


def _gqa_attention(qt, k, vt):
    b, nh, nb, _, blk = qt.shape
    s = nb * blk
    nkv = nh // GQA_GROUP
    body = functools.partial(_gqa_attn_kernel, n_kblk=nb)
    return pl.pallas_call(
        body,
        grid=(b, nkv, nb),
        in_specs=[pl.BlockSpec((1, GQA_GROUP, 1, HEAD_LANES, blk), lambda bb, kv, qi: (bb, kv, qi, 0, 0)),
                  pl.BlockSpec((1, 1, s, HEAD_LANES), lambda bb, kv, qi: (bb, kv, 0, 0)),
                  pl.BlockSpec((1, 1, nb, HEAD_LANES, blk), lambda bb, kv, qi: (bb, kv, 0, 0, 0))],
        out_specs=pl.BlockSpec((1, blk, GQA_GROUP * HEAD_LANES), lambda bb, kv, qi: (bb, qi, kv)),
        out_shape=jax.ShapeDtypeStruct((b, s, nh * HEAD_LANES), BF16),
        scratch_shapes=[pltpu.VMEM((GQA_LOOKAHEAD, blk, blk), F32),
                        pltpu.VMEM((GQA_GROUP, 1, blk), F32),
                        pltpu.VMEM((GQA_GROUP, 1, blk), F32),
                        pltpu.VMEM((GQA_GROUP, 1, blk), F32),
                        pltpu.VMEM((GQA_GROUP, HEAD_LANES, blk), F32),
                        pltpu.SMEM((1,), F32)],
        compiler_params=pltpu.CompilerParams(dimension_semantics=("arbitrary",) * 3,
                                             vmem_limit_bytes=VMEM_LIMIT_BYTES),
        name="gqa_attention",
    )(qt, k, vt)


def _merge_kernel(oa_ref, ob_ref, wa_ref, wb_ref, ga_ref, gb_ref, o_ref):
    a = jnp.dot(oa_ref[...], wa_ref[...], preferred_element_type=F32)
    bb = jnp.dot(ob_ref[...], wb_ref[...], preferred_element_type=F32)
    o_ref[...] = (ga_ref[...] * a + gb_ref[...] * bb).astype(o_ref.dtype)


def _merge(oa, ob, wa, wb, gate, tm=1024, tn=1024):
    t = oa.shape[0]
    gb0 = D_MODEL // tn
    return pl.pallas_call(
        _merge_kernel,
        grid=(t // tm, D_MODEL // tn),
        in_specs=[pl.BlockSpec((tm, DIFF_WIDTH), lambda i, j: (i, 0)),
                  pl.BlockSpec((tm, GQA_WIDTH), lambda i, j: (i, 0)),
                  pl.BlockSpec((DIFF_WIDTH, tn), lambda i, j: (0, j)),
                  pl.BlockSpec((GQA_WIDTH, tn), lambda i, j: (0, j)),
                  pl.BlockSpec((tm, tn), lambda i, j: (i, j)),
                  pl.BlockSpec((tm, tn), lambda i, j: (i, gb0 + j))],
        out_specs=pl.BlockSpec((tm, tn), lambda i, j: (i, j)),
        out_shape=jax.ShapeDtypeStruct((t, D_MODEL), BF16),
        compiler_params=_params(2),
        name="merge",
    )(oa, ob, wa, wb, gate, gate)


def _out_proj_kernel(a_ref, w_ref, x_ref, g_ref, gn_ref, o_ref, h_ref):
    half = a_ref.shape[0] // 2
    for r in range(2):
        rows = slice(r * half, (r + 1) * half)
        m = jnp.dot(a_ref[rows, :], w_ref[...], preferred_element_type=F32)
        x = x_ref[rows, :] + _rms(m, g_ref[...])
        o_ref[rows, :] = x
        h_ref[rows, :] = _rms(x, gn_ref[...]).astype(h_ref.dtype)


def _out_proj(a, w, x2d, g, g_next, tm=512):
    t = a.shape[0]
    return pl.pallas_call(
        _out_proj_kernel,
        grid=(t // tm,),
        in_specs=[pl.BlockSpec((tm, D_MODEL), lambda i: (i, 0)),
                  pl.BlockSpec((D_MODEL, D_MODEL), lambda i: (0, 0)),
                  pl.BlockSpec((tm, D_MODEL), lambda i: (i, 0)),
                  pl.BlockSpec((1, D_MODEL), lambda i: (0, 0)),
                  pl.BlockSpec((1, D_MODEL), lambda i: (0, 0))],
        out_specs=[pl.BlockSpec((tm, D_MODEL), lambda i: (i, 0)),
                   pl.BlockSpec((tm, D_MODEL), lambda i: (i, 0))],
        out_shape=[jax.ShapeDtypeStruct((t, D_MODEL), F32),
                   jax.ShapeDtypeStruct((t, D_MODEL), BF16)],
        compiler_params=_params(1),
        name="out_proj",
    )(a, w, x2d, g.reshape(1, D_MODEL), g_next.reshape(1, D_MODEL))


def _swiglu_kernel(h_ref, wg_ref, wu_ref, o_ref):
    h = h_ref[...]
    gte = jnp.dot(h, wg_ref[...], preferred_element_type=F32)
    up = jnp.dot(h, wu_ref[...], preferred_element_type=F32)
    o_ref[...] = (jax.nn.silu(gte) * up).astype(o_ref.dtype)


def _swiglu(h2d, w_gate_up, tm=1024, tn=512):
    t = h2d.shape[0]
    ub0 = D_FF // tn
    return pl.pallas_call(
        _swiglu_kernel,
        grid=(t // tm, D_FF // tn),
        in_specs=[pl.BlockSpec((tm, D_MODEL), lambda i, j: (i, 0)),
                  pl.BlockSpec((D_MODEL, tn), lambda i, j: (0, j)),
                  pl.BlockSpec((D_MODEL, tn), lambda i, j: (0, ub0 + j))],
        out_specs=pl.BlockSpec((tm, tn), lambda i, j: (i, j)),
        out_shape=jax.ShapeDtypeStruct((t, D_FF), BF16),
        compiler_params=_params(2),
        name="swiglu",
    )(h2d, w_gate_up, w_gate_up)


def _down_kernel(a_ref, w_ref, x_ref, g_ref, *rest, with_next):
    if with_next:
        gn_ref, o_ref, h_ref = rest
    else:
        (o_ref,) = rest
    m = jnp.dot(a_ref[...], w_ref[...], preferred_element_type=F32)
    x = x_ref[...] + _rms(m, g_ref[...])
    o_ref[...] = x
    if with_next:
        h_ref[...] = _rms(x, gn_ref[...]).astype(h_ref.dtype)


def _down_proj(a, w, x2d, g, g_next=None, tm=256):
    t = a.shape[0]
    with_next = g_next is not None
    row_spec = pl.BlockSpec((tm, D_MODEL), lambda i: (i, 0))
    gain_spec = pl.BlockSpec((1, D_MODEL), lambda i: (0, 0))
    in_specs = [pl.BlockSpec((tm, D_FF), lambda i: (i, 0)),
                pl.BlockSpec((D_FF, D_MODEL), lambda i: (0, 0), pipeline_mode=pl.Buffered(1)),
                row_spec, gain_spec]
    args = [a, w, x2d, g.reshape(1, D_MODEL)]
    out_specs = [row_spec]
    out_shape = [jax.ShapeDtypeStruct((t, D_MODEL), F32)]
    if with_next:
        in_specs.append(gain_spec)
        args.append(g_next.reshape(1, D_MODEL))
        out_specs.append(row_spec)
        out_shape.append(jax.ShapeDtypeStruct((t, D_MODEL), BF16))
    out = pl.pallas_call(
        functools.partial(_down_kernel, with_next=with_next),
        grid=(t // tm,),
        in_specs=in_specs, out_specs=out_specs, out_shape=out_shape,
        compiler_params=_params(1),
        name="down_proj",
    )(*args)
    return (out[0], out[1]) if with_next else (out[0], None)


def _rope_tables(s):
    rows = s // GRID_W
    row = jnp.repeat(jnp.arange(rows, dtype=F32), GRID_W)
    col = jnp.tile(jnp.arange(GRID_W, dtype=F32), rows)
    quarter = GQA_HD // 4
    freqs = ROPE_THETA ** (-jnp.arange(quarter, dtype=F32) / quarter)
    ang_r = row[:, None] * freqs[None, :]
    ang_c = col[:, None] * freqs[None, :]
    cos = jnp.concatenate([jnp.cos(ang_r), jnp.cos(ang_c), jnp.cos(ang_r), jnp.cos(ang_c)], axis=-1)
    sin = jnp.concatenate([-jnp.sin(ang_r), -jnp.sin(ang_c), jnp.sin(ang_r), jnp.sin(ang_c)], axis=-1)
    return cos, sin


def _reorder_head_lanes(a):
    heads = a.reshape(a.shape[:-1] + (a.shape[-1] // GQA_HD, GQA_HD))
    return heads[..., jnp.asarray(_rope_lane_order())].reshape(a.shape)


def _trunk(x, weights):
    b, s, _ = x.shape
    x2d = x.reshape(b * s, D_MODEL)
    cos, sin = _rope_tables(s)
    slopes = jnp.exp2(-8.0 * jnp.arange(1, DIFF_HEADS + 1, dtype=F32) / DIFF_HEADS)
    h = _prenorm(x2d, weights[0][0])
    for l in range(DEPTH):
        (mix_pre_g, w_in, diff_lambda, diff_subln_g, gqa_qnorm_g, gqa_knorm_g, w_proj_a, w_proj_b,
         w_out, mix_post_g, ffn_pre_g, w_gate_up, w_down, ffn_post_g) = [w[l] for w in weights]
        lambda_init = 0.8 - 0.6 * math.exp(-0.3 * l)
        dqt = _proj_heads(h, w_in, b, s, OFF_DQ, DIFF_QK_WIDTH, transpose=True,
                          scale=DIFF_HD ** -0.5 * LOG2E)
        dk = _proj_heads(h, w_in, b, s, OFF_DK, DIFF_QK_WIDTH, transpose=False)
        dvt = _proj_heads(h, w_in, b, s, OFF_DV, DIFF_WIDTH, transpose=True)
        o_a = _diff_attention(dqt, dk, dvt, slopes, diff_lambda, diff_subln_g, lambda_init)
        w_rope = _reorder_head_lanes(w_in[:, OFF_GQ:OFF_GV])
        gqt = _proj_heads(h, w_rope, b, s, 0, GQA_WIDTH, transpose=True, scale=GQA_HD ** -0.5 * LOG2E,
                          rope=(_reorder_head_lanes(gqa_qnorm_g), cos, sin))
        gk = _proj_heads(h, w_rope, b, s, GQA_WIDTH, GQA_KV_WIDTH, transpose=False,
                         rope=(_reorder_head_lanes(gqa_knorm_g), cos, sin))
        gvt = _proj_heads(h, w_in, b, s, OFF_GV, GQA_KV_WIDTH, transpose=True)
        o_b = _gqa_attention(gqt, gk, gvt)
        gate = _proj_gate(h, w_in[:, OFF_GATE:])
        merged = _merge(o_a.reshape(b * s, DIFF_WIDTH), o_b.reshape(b * s, GQA_WIDTH),
                        w_proj_a, w_proj_b, gate)
        x2d, h = _out_proj(merged, w_out, x2d, mix_post_g, ffn_pre_g)
        act = _swiglu(h, w_gate_up)
        next_pre_g = weights[0][l + 1] if l + 1 < DEPTH else None
        x2d, h = _down_proj(act, w_down, x2d, ffn_post_g, next_pre_g)
    return x2d.reshape(b, s, D_MODEL)


def kernel(x_prompt, x_sample, mix_pre_g, w_in, diff_lambda, diff_subln_g, gqa_qnorm_g, gqa_knorm_g,
           w_proj_a, w_proj_b, w_out, mix_post_g, ffn_pre_g, w_gate_up, w_down, ffn_post_g):
    weights = (mix_pre_g, w_in.astype(BF16), diff_lambda, diff_subln_g, gqa_qnorm_g, gqa_knorm_g,
               w_proj_a.astype(BF16), w_proj_b.astype(BF16), w_out.astype(BF16), mix_post_g,
               ffn_pre_g, w_gate_up.astype(BF16), w_down.astype(BF16), ffn_post_g)
    return (_trunk(x_prompt, weights), _trunk(x_sample, weights))
```

```python
import functools
import math

import jax
import jax.numpy as jnp
from jax import lax
from jax.experimental import pallas as pl
from jax.experimental.pallas import tpu as pltpu

D_MODEL = 2048
DEPTH = 2
GRID_W = 64
EPS = 1e-6
DIFF_HEADS = 8
DIFF_HD = 64
DIFF_V_HD = 128
DIFF_QK_WIDTH = 1024
DIFF_WIDTH = 1024
GQA_HEADS = 8
GQA_KV_HEADS = 2
GQA_GROUP = GQA_HEADS // GQA_KV_HEADS
GQA_HD = 128
GQA_WIDTH = 1024
GQA_KV_WIDTH = 256
ROPE_THETA = 10000.0
GATE_WIDTH = 2 * D_MODEL
D_FF = 5632

HEAD_LANES = 128
OFF_DQ = 0
OFF_DK = OFF_DQ + DIFF_QK_WIDTH
OFF_DV = OFF_DK + DIFF_QK_WIDTH
OFF_GQ = OFF_DV + DIFF_WIDTH
OFF_GK = OFF_GQ + GQA_WIDTH
OFF_GV = OFF_GK + GQA_KV_WIDTH
OFF_GATE = OFF_GV + GQA_KV_WIDTH

VMEM_LIMIT_BYTES = 52 * 1024 * 1024
ATT_BLK = 512
DIFF_LOOKAHEAD = 1
DIFF_QBLOCKS = 1
GQA_LOOKAHEAD = 1
LOG2E = math.log2(math.e)

BF16 = jnp.bfloat16
F32 = jnp.float32


def _params(n_grid):
    return pltpu.CompilerParams(
        dimension_semantics=("parallel",) * (n_grid - 1) + ("arbitrary",),
        vmem_limit_bytes=VMEM_LIMIT_BYTES)


def _rms(x, g):
    ms = jnp.mean(x * x, axis=-1, keepdims=True)
    return x * lax.rsqrt(ms + EPS) * g


def _prenorm_kernel(x_ref, g_ref, o_ref):
    o_ref[...] = _rms(x_ref[...], g_ref[...]).astype(o_ref.dtype)


def _prenorm(x2d, g, tm=1024):
    t = x2d.shape[0]
    return pl.pallas_call(
        _prenorm_kernel,
        grid=(t // tm,),
        in_specs=[pl.BlockSpec((tm, D_MODEL), lambda i: (i, 0)),
                  pl.BlockSpec((1, D_MODEL), lambda i: (0, 0))],
        out_specs=pl.BlockSpec((tm, D_MODEL), lambda i: (i, 0)),
        out_shape=jax.ShapeDtypeStruct((t, D_MODEL), BF16),
        compiler_params=_params(1),
        name="prenorm",
    )(x2d, g.reshape(1, D_MODEL))


def _rope_lane_order():
    quarter = GQA_HD // 4
    return [i + off for off in (0, 2 * quarter, quarter, 3 * quarter) for i in range(quarter)]


def _rope(y, cos, sin_signed):
    return y * cos + pltpu.roll(y, GQA_HD // 2, 1) * sin_signed


def _head_mean_square(y):
    sq = (y * y).astype(BF16)
    ones = jnp.ones((HEAD_LANES, HEAD_LANES), BF16)
    return jnp.dot(sq, ones, preferred_element_type=F32) * (1.0 / HEAD_LANES)


def _store_head(o_ref, hh, y, transpose):
    if transpose:
        for c in range(y.shape[0] // ATT_BLK):
            o_ref[0, hh, c] = y[c * ATT_BLK:(c + 1) * ATT_BLK].T.astype(o_ref.dtype)
    else:
        o_ref[0, hh] = y.astype(o_ref.dtype)


def _seg_heads_kernel(h_ref, w_ref, o_ref, *, scale, transpose, n_heads):
    acc = jnp.dot(h_ref[...], w_ref[...], preferred_element_type=F32)
    for hh in range(n_heads):
        y = acc[:, hh * HEAD_LANES:(hh + 1) * HEAD_LANES]
        if scale != 1.0:
            y = y * scale
        _store_head(o_ref, hh, y, transpose)


def _seg_rope_kernel(h_ref, w_ref, g_ref, cos_ref, sin_ref, o_ref, *, scale, transpose, n_heads):
    acc = jnp.dot(h_ref[...], w_ref[...], preferred_element_type=F32)
    cos = cos_ref[...]
    sin = sin_ref[...]
    g = g_ref[...]
    for hh in range(n_heads):
        y = acc[:, hh * HEAD_LANES:(hh + 1) * HEAD_LANES]
        y = y * lax.rsqrt(_head_mean_square(y) + EPS) * g
        y = _rope(y, cos, sin)
        if scale != 1.0:
            y = y * scale
        _store_head(o_ref, hh, y, transpose)


def _seg_gate_kernel(h_ref, w_ref, o_ref):
    acc = jnp.dot(h_ref[...], w_ref[...], preferred_element_type=F32)
    o_ref[...] = jax.nn.sigmoid(acc).astype(o_ref.dtype)


def _proj_heads(h2d, w_in, b, s, col0, ncols, *, transpose, scale=1.0, rope=None, tm=1024, tn=1024):
    tn = min(tn, ncols)
    n_heads_blk = tn // HEAD_LANES
    n_heads = ncols // HEAD_LANES
    nsb = s // tm
    grid = (b * nsb, ncols // tn)
    cb0 = col0 // tn
    assert col0 % tn == 0 and s % tm == 0
    in_specs = [pl.BlockSpec((tm, D_MODEL), lambda i, j: (i, 0)),
                pl.BlockSpec((D_MODEL, tn), lambda i, j: (0, cb0 + j))]
    args = [h2d, w_in]
    if rope is None:
        body = functools.partial(_seg_heads_kernel, scale=scale, transpose=transpose, n_heads=n_heads_blk)
    else:
        gain, cos, sin = rope
        in_specs += [pl.BlockSpec((1, HEAD_LANES), lambda i, j: (0, 0)),
                     pl.BlockSpec((tm, HEAD_LANES), lambda i, j: (i % nsb, 0)),
                     pl.BlockSpec((tm, HEAD_LANES), lambda i, j: (i % nsb, 0))]
        args += [gain.reshape(1, HEAD_LANES), cos, sin]
        body = functools.partial(_seg_rope_kernel, scale=scale, transpose=transpose, n_heads=n_heads_blk)
    if transpose:
        out_shape = (b, n_heads, s // ATT_BLK, HEAD_LANES, ATT_BLK)
        out_spec = pl.BlockSpec((1, n_heads_blk, tm // ATT_BLK, HEAD_LANES, ATT_BLK),
                                lambda i, j: (i // nsb, j, i % nsb, 0, 0))
    else:
        out_shape = (b, n_heads, s, HEAD_LANES)
        out_spec = pl.BlockSpec((1, n_heads_blk, tm, HEAD_LANES), lambda i, j: (i // nsb, j, i % nsb, 0))
    return pl.pallas_call(
        body, grid=grid, in_specs=in_specs, out_specs=out_spec,
        out_shape=jax.ShapeDtypeStruct(out_shape, BF16),
        compiler_params=_params(2),
        name="proj_heads",
    )(*args)


def _proj_gate(h2d, w_gate, tm=1024, tn=1024):
    t = h2d.shape[0]
    return pl.pallas_call(
        _seg_gate_kernel,
        grid=(t // tm, GATE_WIDTH // tn),
        in_specs=[pl.BlockSpec((tm, D_MODEL), lambda i, j: (i, 0)),
                  pl.BlockSpec((D_MODEL, tn), lambda i, j: (0, j))],
        out_specs=pl.BlockSpec((tm, tn), lambda i, j: (i, j)),
        out_shape=jax.ShapeDtypeStruct((t, GATE_WIDTH), F32),
        compiler_params=_params(2),
        name="proj_gate",
    )(h2d, w_gate)


BOUNDED_STABILISER_MAX = 32.0
NORM_MARGIN = 1.02
EXP2_UNDERFLOW = 127.0
FEATURE_ROWS = 16


def _online_softmax_step(s, c, vt, m_ref, l_ref, acc_ref, idx):
    m_old = m_ref[idx]
    m_new = jnp.maximum(m_old, jnp.max(s, axis=0, keepdims=True) + c)
    p = jnp.exp2(s - (m_new - c))
    alpha = jnp.exp2(m_old - m_new)
    l_ref[idx] = alpha * l_ref[idx] + jnp.sum(p, axis=0, keepdims=True)
    acc_ref[idx] = alpha * acc_ref[idx] + jnp.dot(vt, p.astype(BF16), preferred_element_type=F32)
    m_ref[idx] = m_new


def _bounded_softmax_step(t, vt, l_ref, acc_ref, idx):
    p = jnp.exp2(t)
    l_ref[idx] = l_ref[idx] + jnp.sum(p, axis=0, keepdims=True)
    acc_ref[idx] = acc_ref[idx] + jnp.dot(vt, p.astype(BF16), preferred_element_type=F32)


def _sweep_key_blocks(lo, hi, keys, values_t, score, step, n_chains, s_s, unroll=1,
                      block_of=lambda t: t, after_prime=None):
    ahead = s_s.shape[0]
    j_first = block_of(lo)
    k_first = keys(j_first)
    for i in range(ahead):
        s_s[i] = score(k_first, j_first, i)
    if after_prime is not None:
        after_prime()

    def key_blocks(t0, count):
        queue = [s_s[i] for i in range(ahead)]
        for dt in range(count):
            j = block_of(t0 + dt)
            j_next = block_of(jnp.minimum(t0 + dt + 1, hi - 1))
            k = keys(j)
            k_next = keys(j_next)
            vt = values_t(j)
            for ci in range(n_chains):
                nxt = ci + ahead
                queue.append(score(k, j, nxt) if nxt < n_chains else score(k_next, j_next, nxt - n_chains))
                step(queue.pop(0), ci, j, vt)
        for i in range(ahead):
            s_s[i] = queue[i]

    sizes = unroll if isinstance(unroll, tuple) else tuple(dict.fromkeys((unroll, 1)))
    pos = lo
    for size in sizes:
        n_groups = (hi - pos) // size
        if isinstance(n_groups, int) and n_groups == 0:
            continue

        def group(i, carry, pos=pos, size=size):
            key_blocks(pos + i * size, size)
            return carry

        lax.fori_loop(0, n_groups, group, 0)
        pos = pos + n_groups * size


def _split_bf16x3(x):
    p0 = x.astype(BF16).astype(F32)
    p1 = (x - p0).astype(BF16).astype(F32)
    p2 = (x - p0 - p1).astype(BF16).astype(F32)
    return p0, p1, p2


def _column_norms(x_t):
    x = x_t.astype(F32)
    return jnp.sqrt(jnp.sum(x * x, axis=0, keepdims=True))


def _diff_attn_kernel(slopes_ref, windows_ref, lam_ref, g_ref, qt_ref, k_ref, vt_ref, o_ref,
                      q_s, bias_s, feat_s, s_s, u_s, m_s, l_s, acc_s, kmax_s, *, lambda_init, n_kblk):
    blk = ATT_BLK
    n_chains = 2 * DIFF_QBLOCKS
    h = pl.program_id(1)
    qi = pl.program_id(2)
    q0 = qi * DIFF_QBLOCKS
    slope = slopes_ref[h] * LOG2E
    half = DIFF_HD

    def keys(j):
        return k_ref[0, 0, pl.ds(pl.multiple_of(j * blk, blk), blk), :]

    @pl.when(qi == 0)
    def _head_init():
        rel = (lax.broadcasted_iota(jnp.int32, (blk, blk), 0)
               - lax.broadcasted_iota(jnp.int32, (blk, blk), 1)).astype(F32)
        bias_s[0] = slope * rel
        bias_s[1] = -slope * jnp.abs(rel)
        bias_s[2] = -slope * rel

        lane = lax.broadcasted_iota(jnp.int32, (blk, HEAD_LANES), 1)
        key_offset = lax.broadcasted_iota(jnp.int32, (blk, HEAD_LANES), 0).astype(F32)
        for side, sign in ((0, 1.0), (1, -1.0)):
            a0, a1, a2 = _split_bf16x3((sign * slope) * key_offset)
            feat_s[side] = jnp.where(lane == 0, a0, jnp.where(lane == 1, a1, jnp.where(lane == 2, a2,
                                     jnp.where(lane < 6, 1.0, 0.0)))).astype(BF16)

        first_map = lax.broadcasted_iota(jnp.int32, (blk, HEAD_LANES), 1) < half

        def norm_block(j, best):
            kb = keys(j).astype(F32)
            sq = kb * kb
            n0 = jnp.sqrt(jnp.sum(jnp.where(first_map, sq, 0.0), axis=1, keepdims=True))
            n1 = jnp.sqrt(jnp.sum(jnp.where(first_map, 0.0, sq), axis=1, keepdims=True))
            return jnp.maximum(best[0], jnp.max(n0)), jnp.maximum(best[1], jnp.max(n1))

        kmax = lax.fori_loop(0, n_kblk, norm_block, (jnp.float32(0.0), jnp.float32(0.0)))
        kmax_s[0] = kmax[0]
        kmax_s[1] = kmax[1]

    zeros = jnp.zeros((half, blk), BF16)
    for qb in range(DIFF_QBLOCKS):
        q_s[2 * qb, 0:half, :] = qt_ref[0, 0, qb, 0:half, :]
        q_s[2 * qb, half:, :] = zeros
        q_s[2 * qb + 1, 0:half, :] = zeros
        q_s[2 * qb + 1, half:, :] = qt_ref[0, 0, qb, half:, :]
    m_s[...] = jnp.full(m_s.shape, -jnp.inf, F32)
    l_s[...] = jnp.zeros(l_s.shape, F32)
    acc_s[...] = jnp.zeros(acc_s.shape, F32)

    u_max = jnp.float32(0.0)
    for ci in range(n_chains):
        u = _column_norms(q_s[ci]) * (kmax_s[ci % 2] * NORM_MARGIN)
        u_s[ci] = u
        u_max = jnp.maximum(u_max, jnp.max(u))
    bounded = u_max <= BOUNDED_STABILISER_MAX

    def score(kblk, j, ci):
        return jnp.dot(kblk, q_s[ci], preferred_element_type=F32)

    slope_row = jnp.full((1, blk), slopes_ref[h], F32) * LOG2E

    def origin_gap(j, qblk):
        return jnp.full((1, blk), jnp.abs(j - qblk) * blk, jnp.int32).astype(F32)

    @pl.when(bounded)
    def _bounded_sweep():
        w = windows_ref[h]
        lo = jnp.maximum(q0 - w, 0)
        hi = jnp.minimum(q0 + 1 + w, n_kblk)

        def block_of(t):
            j = lo + t
            return jnp.minimum(j + (j >= q0).astype(jnp.int32), n_kblk - 1)

        query_offset = lax.broadcasted_iota(jnp.int32, (1, blk), 1).astype(F32)
        feature_row = lax.broadcasted_iota(jnp.int32, (FEATURE_ROWS, blk), 0)
        rhs_pad = jnp.zeros((HEAD_LANES - FEATURE_ROWS, blk), BF16)

        def feature_score(kblk, j, ci):
            right = j > q0
            signed_slope = jnp.where(right, -slope_row, slope_row)
            row = -signed_slope * query_offset - slope_row * origin_gap(j, q0) - u_s[ci]
            r0, r1, r2 = _split_bf16x3(row)
            feats = jnp.where(feature_row < 3, 1.0,
                              jnp.where(feature_row == 3, r0,
                                        jnp.where(feature_row == 4, r1,
                                                  jnp.where(feature_row == 5, r2, 0.0))))
            rhs = jnp.concatenate([q_s[ci], feats.astype(BF16), rhs_pad], axis=0)
            lhs = jnp.concatenate([kblk, feat_s[right.astype(jnp.int32)]], axis=1)
            return jnp.dot(lhs, rhs, preferred_element_type=F32)

        def own_block():
            kd = keys(q0)
            vt = vt_ref[0, 0, q0]
            for ci in range(n_chains):
                s = jnp.dot(kd, q_s[ci], preferred_element_type=F32)
                _bounded_softmax_step(s + bias_s[1] - u_s[ci], vt, l_s, acc_s, ci)

        def step(s, ci, j, vt):
            _bounded_softmax_step(s, vt, l_s, acc_s, ci)

        _sweep_key_blocks(0, hi - lo - 1, keys, lambda j: vt_ref[0, 0, j], feature_score, step, n_chains, s_s,
                          unroll=(8, 4, 2, 1), block_of=block_of, after_prime=own_block)

    @pl.when(jnp.logical_not(bounded))
    def _online_sweep():
        def step(s, ci, j, vt):
            side = jnp.where(j < q0, 0, jnp.where(j == q0, 1, 2))
            _online_softmax_step(s + bias_s[side], -slope_row * origin_gap(j, q0), vt, m_s, l_s, acc_s, ci)

        _sweep_key_blocks(0, n_kblk, keys, lambda j: vt_ref[0, 0, j], score, step, n_chains, s_s)

    dl = lam_ref[...]
    lam = (jnp.exp(jnp.sum(dl[0:1] * dl[1:2], axis=-1, keepdims=True))
           - jnp.exp(jnp.sum(dl[2:3] * dl[3:4], axis=-1, keepdims=True)) + lambda_init)
    for qb in range(DIFF_QBLOCKS):
        c0, c1 = 2 * qb, 2 * qb + 1
        o = acc_s[c0] * (1.0 / l_s[c0]) - lam * (acc_s[c1] * (1.0 / l_s[c1]))
        ms = jnp.mean(o * o, axis=0, keepdims=True)
        y = o * lax.rsqrt(ms + EPS) * g_ref[...] * (1.0 - lambda_init)
        o_ref[0, qb * blk:(qb + 1) * blk, :] = y.T.astype(o_ref.dtype)


def _alibi_windows(slopes, blk):
    dist = EXP2_UNDERFLOW / (slopes * LOG2E)
    return (jnp.floor((dist - 1.0) / blk) + 1.0).astype(jnp.int32)


def _diff_attention(qt, k, vt, slopes, diff_lambda, subln_g, lambda_init):
    b, nh, nb, _, blk = qt.shape
    s = nb * blk
    body = functools.partial(_diff_attn_kernel, lambda_init=lambda_init, n_kblk=nb)
    qbs = DIFF_QBLOCKS
    assert qbs == 1, "the bounded sweep folds the bias of one query block into the score matmul"
    n_chains = 2 * qbs
    return pl.pallas_call(
        body,
        grid=(b, nh, nb // qbs),
        in_specs=[pl.BlockSpec(memory_space=pltpu.SMEM),
                  pl.BlockSpec(memory_space=pltpu.SMEM),
                  pl.BlockSpec((4, DIFF_HD), lambda bb, hh, qi: (0, 0)),
                  pl.BlockSpec((DIFF_V_HD, 1), lambda bb, hh, qi: (0, 0)),
                  pl.BlockSpec((1, 1, qbs, HEAD_LANES, blk), lambda bb, hh, qi: (bb, hh, qi, 0, 0)),
                  pl.BlockSpec((1, 1, s, HEAD_LANES), lambda bb, hh, qi: (bb, hh, 0, 0)),
                  pl.BlockSpec((1, 1, nb, HEAD_LANES, blk), lambda bb, hh, qi: (bb, hh, 0, 0, 0))],
        out_specs=pl.BlockSpec((1, qbs * blk, HEAD_LANES), lambda bb, hh, qi: (bb, qi, hh)),
        out_shape=jax.ShapeDtypeStruct((b, s, nh * HEAD_LANES), BF16),
        scratch_shapes=[pltpu.VMEM((n_chains, HEAD_LANES, blk), BF16),
                        pltpu.VMEM((3, blk, blk), F32),
                        pltpu.VMEM((2, blk, HEAD_LANES), BF16),
                        pltpu.VMEM((DIFF_LOOKAHEAD, blk, blk), F32),
                        pltpu.VMEM((n_chains, 1, blk), F32),
                        pltpu.VMEM((n_chains, 1, blk), F32),
                        pltpu.VMEM((n_chains, 1, blk), F32),
                        pltpu.VMEM((n_chains, HEAD_LANES, blk), F32),
                        pltpu.SMEM((2,), F32)],
        compiler_params=pltpu.CompilerParams(dimension_semantics=("arbitrary",) * 3,
                                             vmem_limit_bytes=VMEM_LIMIT_BYTES),
        name="diff_attention",
    )(slopes, _alibi_windows(slopes, blk), diff_lambda, subln_g.reshape(DIFF_V_HD, 1), qt, k, vt)


def _gqa_attn_kernel(qt_ref, k_ref, vt_ref, o_ref, s_s, u_s, m_s, l_s, acc_s, kmax_s, *, n_kblk):
    blk = ATT_BLK
    qi = pl.program_id(2)

    def keys(j):
        return k_ref[0, 0, pl.ds(pl.multiple_of(j * blk, blk), blk), :]

    @pl.when(qi == 0)
    def _kv_head_init():
        def norm_block(j, best):
            kb = keys(j).astype(F32)
            return jnp.maximum(best, jnp.max(jnp.sqrt(jnp.sum(kb * kb, axis=1, keepdims=True))))

        kmax_s[0] = lax.fori_loop(0, n_kblk, norm_block, jnp.float32(0.0))

    m_s[...] = jnp.full(m_s.shape, -jnp.inf, F32)
    l_s[...] = jnp.zeros(l_s.shape, F32)
    acc_s[...] = jnp.zeros(acc_s.shape, F32)

    u_max = jnp.float32(0.0)
    for g in range(GQA_GROUP):
        u = _column_norms(qt_ref[0, g, 0]) * (kmax_s[0] * NORM_MARGIN)
        u_s[g] = u
        u_max = jnp.maximum(u_max, jnp.max(u))
    bounded = u_max <= BOUNDED_STABILISER_MAX

    def score(kblk, j, g):
        return jnp.dot(kblk, qt_ref[0, g, 0], preferred_element_type=F32)

    @pl.when(bounded)
    def _bounded_sweep():
        def step(s, g, j, vt):
            _bounded_softmax_step(s - u_s[g], vt, l_s, acc_s, g)

        _sweep_key_blocks(0, n_kblk, keys, lambda j: vt_ref[0, 0, j], score, step, GQA_GROUP, s_s, unroll=4)

    @pl.when(jnp.logical_not(bounded))
    def _online_sweep():
        def step(s, g, j, vt):
            _online_softmax_step(s, 0.0, vt, m_s, l_s, acc_s, g)

        _sweep_key_blocks(0, n_kblk, keys, lambda j: vt_ref[0, 0, j], score, step, GQA_GROUP, s_s)

    for g in range(GQA_GROUP):
        o = acc_s[g] * (1.0 / l_s[g])
        o_ref[0, :, g * HEAD_LANES:(g + 1) * HEAD_LANES] = o.T.astype(o_ref.dtype)


def _gqa_attention(qt, k, vt):
    b, nh, nb, _, blk = qt.shape
    s = nb * blk
    nkv = nh // GQA_GROUP
    body = functools.partial(_gqa_attn_kernel, n_kblk=nb)
    return pl.pallas_call(
        body,
        grid=(b, nkv, nb),
        in_specs=[pl.BlockSpec((1, GQA_GROUP, 1, HEAD_LANES, blk), lambda bb, kv, qi: (bb, kv, qi, 0, 0)),
                  pl.BlockSpec((1, 1, s, HEAD_LANES), lambda bb, kv, qi: (bb, kv, 0, 0)),
                  pl.BlockSpec((1, 1, nb, HEAD_LANES, blk), lambda bb, kv, qi: (bb, kv, 0, 0, 0))],
        out_specs=pl.BlockSpec((1, blk, GQA_GROUP * HEAD_LANES), lambda bb, kv, qi: (bb, qi, kv)),
        out_shape=jax.ShapeDtypeStruct((b, s, nh * HEAD_LANES), BF16),
        scratch_shapes=[pltpu.VMEM((GQA_LOOKAHEAD, blk, blk), F32),
                        pltpu.VMEM((GQA_GROUP, 1, blk), F32),
                        pltpu.VMEM((GQA_GROUP, 1, blk), F32),
                        pltpu.VMEM((GQA_GROUP, 1, blk), F32),
                        pltpu.VMEM((GQA_GROUP, HEAD_LANES, blk), F32),
                        pltpu.SMEM((1,), F32)],
        compiler_params=pltpu.CompilerParams(dimension_semantics=("arbitrary",) * 3,
                                             vmem_limit_bytes=VMEM_LIMIT_BYTES),
        name="gqa_attention",
    )(qt, k, vt)


def _merge_kernel(oa_ref, ob_ref, wa_ref, wb_ref, ga_ref, gb_ref, o_ref):
    a = jnp.dot(oa_ref[...], wa_ref[...], preferred_element_type=F32)
    bb = jnp.dot(ob_ref[...], wb_ref[...], preferred_element_type=F32)
    o_ref[...] = (ga_ref[...] * a + gb_ref[...] * bb).astype(o_ref.dtype)


def _merge(oa, ob, wa, wb, gate, tm=1024, tn=1024):
    t = oa.shape[0]
    gb0 = D_MODEL // tn
    return pl.pallas_call(
        _merge_kernel,
        grid=(t // tm, D_MODEL // tn),
        in_specs=[pl.BlockSpec((tm, DIFF_WIDTH), lambda i, j: (i, 0)),
                  pl.BlockSpec((tm, GQA_WIDTH), lambda i, j: (i, 0)),
                  pl.BlockSpec((DIFF_WIDTH, tn), lambda i, j: (0, j)),
                  pl.BlockSpec((GQA_WIDTH, tn), lambda i, j: (0, j)),
                  pl.BlockSpec((tm, tn), lambda i, j: (i, j)),
                  pl.BlockSpec((tm, tn), lambda i, j: (i, gb0 + j))],
        out_specs=pl.BlockSpec((tm, tn), lambda i, j: (i, j)),
        out_shape=jax.ShapeDtypeStruct((t, D_MODEL), BF16),
        compiler_params=_params(2),
        name="merge",
    )(oa, ob, wa, wb, gate, gate)


def _out_proj_kernel(a_ref, w_ref, x_ref, g_ref, gn_ref, o_ref, h_ref):
    half = a_ref.shape[0] // 2
    for r in range(2):
        rows = slice(r * half, (r + 1) * half)
        m = jnp.dot(a_ref[rows, :], w_ref[...], preferred_element_type=F32)
        x = x_ref[rows, :] + _rms(m, g_ref[...])
        o_ref[rows, :] = x
        h_ref[rows, :] = _rms(x, gn_ref[...]).astype(h_ref.dtype)


def _out_proj(a, w, x2d, g, g_next, tm=512):
    t = a.shape[0]
    return pl.pallas_call(
        _out_proj_kernel,
        grid=(t // tm,),
        in_specs=[pl.BlockSpec((tm, D_MODEL), lambda i: (i, 0)),
                  pl.BlockSpec((D_MODEL, D_MODEL), lambda i: (0, 0)),
                  pl.BlockSpec((tm, D_MODEL), lambda i: (i, 0)),
                  pl.BlockSpec((1, D_MODEL), lambda i: (0, 0)),
                  pl.BlockSpec((1, D_MODEL), lambda i: (0, 0))],
        out_specs=[pl.BlockSpec((tm, D_MODEL), lambda i: (i, 0)),
                   pl.BlockSpec((tm, D_MODEL), lambda i: (i, 0))],
        out_shape=[jax.ShapeDtypeStruct((t, D_MODEL), F32),
                   jax.ShapeDtypeStruct((t, D_MODEL), BF16)],
        compiler_params=_params(1),
        name="out_proj",
    )(a, w, x2d, g.reshape(1, D_MODEL), g_next.reshape(1, D_MODEL))


def _swiglu_kernel(h_ref, wg_ref, wu_ref, o_ref):
    h = h_ref[...]
    gte = jnp.dot(h, wg_ref[...], preferred_element_type=F32)
    up = jnp.dot(h, wu_ref[...], preferred_element_type=F32)
    o_ref[...] = (jax.nn.silu(gte) * up).astype(o_ref.dtype)


def _swiglu(h2d, w_gate_up, tm=1024, tn=512):
    t = h2d.shape[0]
    ub0 = D_FF // tn
    return pl.pallas_call(
        _swiglu_kernel,
        grid=(t // tm, D_FF // tn),
        in_specs=[pl.BlockSpec((tm, D_MODEL), lambda i, j: (i, 0)),
                  pl.BlockSpec((D_MODEL, tn), lambda i, j: (0, j)),
                  pl.BlockSpec((D_MODEL, tn), lambda i, j: (0, ub0 + j))],
        out_specs=pl.BlockSpec((tm, tn), lambda i, j: (i, j)),
        out_shape=jax.ShapeDtypeStruct((t, D_FF), BF16),
        compiler_params=_params(2),
        name="swiglu",
    )(h2d, w_gate_up, w_gate_up)


def _down_kernel(a_ref, w_ref, x_ref, g_ref, *rest, with_next):
    if with_next:
        gn_ref, o_ref, h_ref = rest
    else:
        (o_ref,) = rest
    m = jnp.dot(a_ref[...], w_ref[...], preferred_element_type=F32)
    x = x_ref[...] + _rms(m, g_ref[...])
    o_ref[...] = x
    if with_next:
        h_ref[...] = _rms(x, gn_ref[...]).astype(h_ref.dtype)


def _down_proj(a, w, x2d, g, g_next=None, tm=256):
    t = a.shape[0]
    with_next = g_next is not None
    row_spec = pl.BlockSpec((tm, D_MODEL), lambda i: (i, 0))
    gain_spec = pl.BlockSpec((1, D_MODEL), lambda i: (0, 0))
    in_specs = [pl.BlockSpec((tm, D_FF), lambda i: (i, 0)),
                pl.BlockSpec((D_FF, D_MODEL), lambda i: (0, 0), pipeline_mode=pl.Buffered(1)),
                row_spec, gain_spec]
    args = [a, w, x2d, g.reshape(1, D_MODEL)]
    out_specs = [row_spec]
    out_shape = [jax.ShapeDtypeStruct((t, D_MODEL), F32)]
    if with_next:
        in_specs.append(gain_spec)
        args.append(g_next.reshape(1, D_MODEL))
        out_specs.append(row_spec)
        out_shape.append(jax.ShapeDtypeStruct((t, D_MODEL), BF16))
    out = pl.pallas_call(
        functools.partial(_down_kernel, with_next=with_next),
        grid=(t // tm,),
        in_specs=in_specs, out_specs=out_specs, out_shape=out_shape,
        compiler_params=_params(1),
        name="down_proj",
    )(*args)
    return (out[0], out[1]) if with_next else (out[0], None)


def _rope_tables(s):
    rows = s // GRID_W
    row = jnp.repeat(jnp.arange(rows, dtype=F32), GRID_W)
    col = jnp.tile(jnp.arange(GRID_W, dtype=F32), rows)
    quarter = GQA_HD // 4
    freqs = ROPE_THETA ** (-jnp.arange(quarter, dtype=F32) / quarter)
    ang_r = row[:, None] * freqs[None, :]
    ang_c = col[:, None] * freqs[None, :]
    cos = jnp.concatenate([jnp.cos(ang_r), jnp.cos(ang_c), jnp.cos(ang_r), jnp.cos(ang_c)], axis=-1)
    sin = jnp.concatenate([-jnp.sin(ang_r), -jnp.sin(ang_c), jnp.sin(ang_r), jnp.sin(ang_c)], axis=-1)
    return cos, sin


def _reorder_head_lanes(a):
    heads = a.reshape(a.shape[:-1] + (a.shape[-1] // GQA_HD, GQA_HD))
    return heads[..., jnp.asarray(_rope_lane_order())].reshape(a.shape)


def _trunk(x, weights):
    b, s, _ = x.shape
    x2d = x.reshape(b * s, D_MODEL)
    cos, sin = _rope_tables(s)
    slopes = jnp.exp2(-8.0 * jnp.arange(1, DIFF_HEADS + 1, dtype=F32) / DIFF_HEADS)
    h = _prenorm(x2d, weights[0][0])
    for l in range(DEPTH):
        (mix_pre_g, w_in, diff_lambda, diff_subln_g, gqa_qnorm_g, gqa_knorm_g, w_proj_a, w_proj_b,
         w_out, mix_post_g, ffn_pre_g, w_gate_up, w_down, ffn_post_g) = [w[l] for w in weights]
        lambda_init = 0.8 - 0.6 * math.exp(-0.3 * l)
        dqt = _proj_heads(h, w_in, b, s, OFF_DQ, DIFF_QK_WIDTH, transpose=True,
                          scale=DIFF_HD ** -0.5 * LOG2E)
        dk = _proj_heads(h, w_in, b, s, OFF_DK, DIFF_QK_WIDTH, transpose=False)
        dvt = _proj_heads(h, w_in, b, s, OFF_DV, DIFF_WIDTH, transpose=True)
        o_a = _diff_attention(dqt, dk, dvt, slopes, diff_lambda, diff_subln_g, lambda_init)
        w_rope = _reorder_head_lanes(w_in[:, OFF_GQ:OFF_GV])
        gqt = _proj_heads(h, w_rope, b, s, 0, GQA_WIDTH, transpose=True, scale=GQA_HD ** -0.5 * LOG2E,
                          rope=(_reorder_head_lanes(gqa_qnorm_g), cos, sin))
        gk = _proj_heads(h, w_rope, b, s, GQA_WIDTH, GQA_KV_WIDTH, transpose=False,
                         rope=(_reorder_head_lanes(gqa_knorm_g), cos, sin))
        gvt = _proj_heads(h, w_in, b, s, OFF_GV, GQA_KV_WIDTH, transpose=True)
        o_b = _gqa_attention(gqt, gk, gvt)
        gate = _proj_gate(h, w_in[:, OFF_GATE:])
        merged = _merge(o_a.reshape(b * s, DIFF_WIDTH), o_b.reshape(b * s, GQA_WIDTH),
                        w_proj_a, w_proj_b, gate)
        x2d, h = _out_proj(merged, w_out, x2d, mix_post_g, ffn_pre_g)
        act = _swiglu(h, w_gate_up)
        next_pre_g = weights[0][l + 1] if l + 1 < DEPTH else None
        x2d, h = _down_proj(act, w_down, x2d, ffn_post_g, next_pre_g)
    return x2d.reshape(b, s, D_MODEL)


def kernel(x_prompt, x_sample, mix_pre_g, w_in, diff_lambda, diff_subln_g, gqa_qnorm_g, gqa_knorm_g,
           w_proj_a, w_proj_b, w_out, mix_post_g, ffn_pre_g, w_gate_up, w_down, ffn_post_g):
    weights = (mix_pre_g, w_in.astype(BF16), diff_lambda, diff_subln_g, gqa_qnorm_g, gqa_knorm_g,
               w_proj_a.astype(BF16), w_proj_b.astype(BF16), w_out.astype(BF16), mix_post_g,
               ffn_pre_g, w_gate_up.astype(BF16), w_down.astype(BF16), ffn_post_g)
    return (_trunk(x_prompt, weights), _trunk(x_sample, weights))
```
